```python
import math
import jax, jax.numpy as jnp
from jax import lax
import numpy as np

D_MODEL = 1024
BATCH = 32
SEQ = 256
DEPTH = 4
DEC_BATCH = 2
DEC_SEQ = 1024
PAST_LEN = 256

GRID_W = 64
GROUP_W = D_MODEL // 4
A_HEADS = 4
A_V_DIM = GROUP_W // A_HEADS
A_QK_DIM = A_V_DIM // 2
B_GROUPS = 4
CHUNK = 128
C_KERNEL = 31
D_GROUPS = 4
D_FF = 4 * D_MODEL
ROPE_BASE = 10000.0
Q_BLOCK = 128
EPS = 1e-6
IN_W = 8 * GROUP_W
SPLITS = (GROUP_W, 2 * GROUP_W, 3 * GROUP_W, 5 * GROUP_W, 7 * GROUP_W)

kernel_name = "hybrid_diffusion_prefix_trunk_step"

F32 = jnp.float32


def rms_norm(x, g):
    xf = x.astype(F32)
    y = xf * lax.rsqrt(jnp.mean(xf * xf, axis=-1, keepdims=True) + EPS)
    return (y * g.astype(F32)).astype(x.dtype)


def layer_norm(x, g, b):
    xf = x.astype(F32)
    mu = jnp.mean(xf, axis=-1, keepdims=True)
    var = jnp.mean(jnp.square(xf - mu), axis=-1, keepdims=True)
    y = (xf - mu) * lax.rsqrt(var + EPS) * g.astype(F32) + b.astype(F32)
    return y.astype(x.dtype)


def axial_rope_tables(n_tokens):
    rows = n_tokens // GRID_W
    row = jnp.repeat(jnp.arange(rows, dtype=F32), GRID_W)
    col = jnp.tile(jnp.arange(GRID_W, dtype=F32), rows)
    nf = A_QK_DIM // 4
    inv = ROPE_BASE ** (-jnp.arange(nf, dtype=F32) / nf)
    ang = jnp.stack([row[:, None] * inv, col[:, None] * inv], axis=1)
    return jnp.cos(ang), jnp.sin(ang)


def apply_rope(x, cos, sin):
    shp = x.shape
    xr = x.astype(F32).reshape(shp[:-1] + (2, 2, A_QK_DIM // 4))
    a, b = xr[..., 0, :], xr[..., 1, :]
    cs, sn = cos[:, None, None], sin[:, None, None]
    rot = jnp.stack([a * cs - b * sn, b * cs + a * sn], axis=-2)
    return rot.reshape(shp).astype(x.dtype)


def diff_attention(q, k, v, lam):
    bn, lq = q.shape[:2]
    nb = lq // Q_BLOCK
    qb = q.reshape((bn, nb, Q_BLOCK) + q.shape[2:]).swapaxes(0, 1)
    kf, vf = k.astype(F32), v.astype(F32)
    scale = A_QK_DIM ** -0.5

    def block(qi):
        s = jnp.einsum('bqhcd,bkhcd->bchqk', qi.astype(F32), kf) * scale
        p = jax.nn.softmax(s, axis=-1)
        w = p[:, 0] - lam * p[:, 1]
        return jnp.einsum('bhqk,bkhd->bqhd', w, vf)

    o = lax.map(block, qb)
    return o.swapaxes(0, 1).reshape(bn, lq, A_HEADS, A_V_DIM)


def spatial_gating(u, v, g, b, w_s, b_s):
    bn, L, _ = u.shape
    vn = layer_norm(v, g, b)
    vc = vn.reshape(bn, L // CHUNK, CHUNK, B_GROUPS, GROUP_W // B_GROUPS)
    mixed = jnp.einsum('gpq,bnqgc->bnpgc', w_s, vc) + b_s.T[None, None, :, :, None]
    return u * mixed.reshape(bn, L, GROUP_W)


def conv_module(a, gate, w_dw, b_dw, g, b):
    h = a * jax.nn.sigmoid(gate)
    y = lax.conv_general_dilated(
        h, w_dw[:, None, :], window_strides=(1,),
        padding=[(C_KERNEL // 2, C_KERNEL // 2)],
        dimension_numbers=('NWC', 'WIO', 'NWC'),
        feature_group_count=GROUP_W) + b_dw
    return jax.nn.silu(layer_norm(y, g, b))


def fourier_mix(z):
    bn, L, _ = z.shape
    zg = z.reshape(bn, L, D_GROUPS, GROUP_W // D_GROUPS).astype(F32)
    f = jnp.fft.fftn(zg, axes=(1, 3), norm='ortho').real
    return f.reshape(bn, L, GROUP_W).astype(z.dtype)


def trunk_layer(x, cond, P, l, rope, ctx_k, ctx_v):
    bn, L, _ = x.shape
    mod = (jax.nn.silu(cond) @ P['w_ada'][l] + P['b_ada'][l])[:, None, :]
    sh1, sc1, gt1, sh2, sc2, gt2 = jnp.split(mod, 6, axis=-1)
    h = rms_norm(x, P['g_attn_norm'][l]) * (1 + sc1) + sh1
    z = h @ P['w_in'][l]
    zq, zk, zv, zb, zc, zd = jnp.split(z, SPLITS, axis=-1)

    q = rms_norm(zq.reshape(bn, L, A_HEADS, 2, A_QK_DIM), P['g_q'][l])
    k = rms_norm(zk.reshape(bn, L, A_HEADS, 2, A_QK_DIM), P['g_k'][l])
    v = zv.reshape(bn, L, A_HEADS, A_V_DIM)
    own_k, own_v = k, v
    if rope is not None:
        q = apply_rope(q, *rope)
        k = apply_rope(k, *rope)
    if ctx_k is not None:
        k = jnp.concatenate([ctx_k.astype(k.dtype), k], axis=1)
        v = jnp.concatenate([ctx_v.astype(v.dtype), v], axis=1)
    lam_init = 0.8 - 0.6 * math.exp(-0.3 * l)
    lam = (jnp.exp(jnp.sum(P['lam_q1'][l].astype(F32) * P['lam_k1'][l].astype(F32)))
           - jnp.exp(jnp.sum(P['lam_q2'][l].astype(F32) * P['lam_k2'][l].astype(F32)))
           + lam_init)
    o_a = rms_norm(diff_attention(q, k, v, lam), P['g_head'][l]) * (1.0 - lam_init)
    o_a = o_a.reshape(bn, L, GROUP_W).astype(x.dtype)

    ub, vb = jnp.split(jax.nn.gelu(zb), 2, axis=-1)
    o_b = spatial_gating(ub, vb, P['g_sg'][l], P['b_sg'][l], P['w_spatial'][l], P['b_spatial'][l])

    ac, gc = jnp.split(zc, 2, axis=-1)
    o_c = conv_module(ac, gc, P['w_dw'][l], P['b_dw'][l], P['g_conv'][l], P['b_conv'][l])

    o_d = fourier_mix(zd)

    mix = jnp.concatenate([o_a, o_b, o_c, o_d], axis=-1) @ P['w_out'][l]
    x = x + gt1 * mix
    h2 = rms_norm(x, P['g_mlp_norm'][l]) * (1 + sc2) + sh2
    ff = jnp.square(jax.nn.relu(h2 @ P['w_ff1'][l])) @ P['w_ff2'][l]
    x = x + gt2 * ff
    return x, own_k, own_v


def setup_inputs(seed: int = 0) -> dict:
    key = jax.random.key(seed)
    ks = jax.random.split(key, 32)
    D = D_MODEL

    def nrm(k, shape, s):
        return jax.random.normal(k, shape, F32) * s

    return {
        'x_prompt': nrm(ks[0], (BATCH, SEQ, D), 1.0),
        'x_sample': nrm(ks[1], (DEC_BATCH, DEC_SEQ, D), 1.0),
        'c': nrm(ks[2], (DEC_BATCH, D), 1.0),
        'cache_k': nrm(ks[3], (DEC_BATCH, DEPTH, PAST_LEN, A_HEADS, 2 * A_QK_DIM), 1.0),
        'cache_v': nrm(ks[4], (DEC_BATCH, DEPTH, PAST_LEN, A_HEADS, A_V_DIM), 1.0),
        'c_ctx': nrm(ks[5], (D,), 1.0),
        'w_ada': nrm(ks[6], (DEPTH, D, 6 * D), D ** -0.5),
        'b_ada': nrm(ks[7], (DEPTH, 6 * D), 0.02),
        'g_attn_norm': 1.0 + nrm(ks[8], (DEPTH, D), 0.05),
        'g_mlp_norm': 1.0 + nrm(ks[9], (DEPTH, D), 0.05),
        'w_in': nrm(ks[10], (DEPTH, D, IN_W), D ** -0.5),
        'g_q': 1.0 + nrm(ks[11], (DEPTH, A_QK_DIM), 0.05),
        'g_k': 1.0 + nrm(ks[12], (DEPTH, A_QK_DIM), 0.05),
        'lam_q1': nrm(ks[13], (DEPTH, A_QK_DIM), 0.1),
        'lam_k1': nrm(ks[14], (DEPTH, A_QK_DIM), 0.1),
        'lam_q2': nrm(ks[15], (DEPTH, A_QK_DIM), 0.1),
        'lam_k2': nrm(ks[16], (DEPTH, A_QK_DIM), 0.1),
        'g_head': 1.0 + nrm(ks[17], (DEPTH, A_V_DIM), 0.05),
        'g_sg': 1.0 + nrm(ks[18], (DEPTH, GROUP_W), 0.05),
        'b_sg': nrm(ks[19], (DEPTH, GROUP_W), 0.02),
        'w_spatial': nrm(ks[20], (DEPTH, B_GROUPS, CHUNK, CHUNK), CHUNK ** -0.5),
        'b_spatial': 1.0 + nrm(ks[21], (DEPTH, B_GROUPS, CHUNK), 0.05),
        'w_dw': nrm(ks[22], (DEPTH, C_KERNEL, GROUP_W), C_KERNEL ** -0.5),
        'b_dw': nrm(ks[23], (DEPTH, GROUP_W), 0.02),
        'g_conv': 1.0 + nrm(ks[24], (DEPTH, GROUP_W), 0.05),
        'b_conv': nrm(ks[25], (DEPTH, GROUP_W), 0.02),
        'w_out': nrm(ks[26], (DEPTH, D, D), D ** -0.5),
        'w_ff1': nrm(ks[27], (DEPTH, D, D_FF), D ** -0.5),
        'w_ff2': nrm(ks[28], (DEPTH, D_FF, D), D_FF ** -0.5),
    }


def reference(x_prompt, x_sample, c, cache_k, cache_v, c_ctx, w_ada, b_ada,
              g_attn_norm, g_mlp_norm, w_in, g_q, g_k, lam_q1, lam_k1, lam_q2, lam_k2,
              g_head, g_sg, b_sg, w_spatial, b_spatial, w_dw, b_dw, g_conv, b_conv,
              w_out, w_ff1, w_ff2):
    P = dict(w_ada=w_ada, b_ada=b_ada, g_attn_norm=g_attn_norm, g_mlp_norm=g_mlp_norm,
             w_in=w_in, g_q=g_q, g_k=g_k, lam_q1=lam_q1, lam_k1=lam_k1, lam_q2=lam_q2,
             lam_k2=lam_k2, g_head=g_head, g_sg=g_sg, b_sg=b_sg, w_spatial=w_spatial,
             b_spatial=b_spatial, w_dw=w_dw, b_dw=b_dw, g_conv=g_conv, b_conv=b_conv,
             w_out=w_out, w_ff1=w_ff1, w_ff2=w_ff2)
    rope = axial_rope_tables(x_sample.shape[1])
    ctx_cond = c_ctx[None, :]
    n_p, l_p = x_prompt.shape[:2]
    n_s, l_c = cache_k.shape[0], cache_k.shape[2]
    xp, xs = x_prompt, x_sample
    k_list, v_list = [], []
    for l in range(DEPTH):
        xp, k_l, v_l = trunk_layer(xp, ctx_cond, P, l, None, None, None)
        k_list.append(k_l.reshape(n_p, l_p, A_HEADS, 2 * A_QK_DIM))
        v_list.append(v_l)
        ck = cache_k[:, l].reshape(n_s, l_c, A_HEADS, 2, A_QK_DIM)
        xs, _, _ = trunk_layer(xs, c, P, l, rope, ck, cache_v[:, l])
    new_k = jnp.stack(k_list, axis=1)
    new_v = jnp.stack(v_list, axis=1)
    return (xp, xs, new_k, new_v)
```

```python
import functools
import math

import numpy as np
import jax
import jax.numpy as jnp
from jax import lax
from jax.experimental import pallas as pl
from jax.experimental.pallas import tpu as pltpu

F32 = jnp.float32
BF16 = jnp.bfloat16

D_MODEL = 1024
DEPTH = 4
GRID_W = 64
GROUP_W = 256
A_HEADS = 4
A_V_DIM = 64
A_QK_DIM = 32
B_GROUPS = 4
CHUNK = 128
C_KERNEL = 31
D_GROUPS = 4
D_FF = 4 * D_MODEL
ROPE_BASE = 10000.0
EPS = 1e-6
IN_W = 8 * GROUP_W

ROW_BLOCK = 256
CONV_PAD = 16
ADA_ROWS = 8
ADA_COLS = 2048
VMEM_LIMIT = 58 * 1024 * 1024


def _dot(a, b):
    return jnp.dot(a, b, preferred_element_type=F32)


def _sigmoid(x):
    return 1.0 / (1.0 + jnp.exp(-x))


def _rms(x, g):
    ms = jnp.mean(x * x, axis=-1, keepdims=True)
    return x * lax.rsqrt(ms + EPS) * g


def _layer_norm(x, g, b):
    mu = jnp.mean(x, axis=-1, keepdims=True)
    xc = x - mu
    var = jnp.mean(xc * xc, axis=-1, keepdims=True)
    return xc * lax.rsqrt(var + EPS) * g + b


def _group_rms(x, gmat, gsize, g):
    ms = _dot((x * x).astype(BF16), gmat) * (1.0 / gsize)
    return x * lax.rsqrt(ms + EPS) * g


def _ada_kernel(cond_ref, w_ref, b_ref, o_ref):
    cnd = cond_ref[...]
    s = (cnd * _sigmoid(cnd)).astype(BF16)
    o_ref[0] = _dot(s, w_ref[0].astype(BF16)) + b_ref[0]


def _ada_modulation(cond, w_ada, b_ada):
    ncol = w_ada.shape[-1]
    return pl.pallas_call(
        _ada_kernel,
        grid=(DEPTH, ncol // ADA_COLS),
        in_specs=[
            pl.BlockSpec((ADA_ROWS, D_MODEL), lambda l, j: (0, 0)),
            pl.BlockSpec((1, D_MODEL, ADA_COLS), lambda l, j: (l, 0, j)),
            pl.BlockSpec((1, 1, ADA_COLS), lambda l, j: (l, 0, j)),
        ],
        out_specs=pl.BlockSpec((1, ADA_ROWS, ADA_COLS), lambda l, j: (l, 0, j)),
        out_shape=jax.ShapeDtypeStruct((DEPTH, ADA_ROWS, ncol), F32),
        compiler_params=pltpu.CompilerParams(
            dimension_semantics=("arbitrary", "arbitrary"),
            vmem_limit_bytes=VMEM_LIMIT),
        name="ada_modulation",
    )(cond, w_ada, b_ada.reshape(DEPTH, 1, ncol))


_V_GQ, _V_GK, _V_GHEAD, _V_GSG, _V_BSG, _V_BDW, _V_GCONV, _V_BCONV = range(8)


def _layer_kernel(*refs, seq_len, past_len, latent, lam_init):
    L = seq_len
    n_blk = L // ROW_BLOCK
    it = iter(refs)
    x_ref = next(it)
    mod_ref = next(it)
    v1024_ref = next(it)
    v256_ref = next(it)
    lam_ref = next(it)
    wsp_ref = next(it)
    bsp_ref = next(it)
    wdw_ref = next(it)
    bdcs_ref = next(it)
    csl_ref = next(it)
    w_in_ref = next(it)
    w_out_ref = next(it)
    w1_ref = next(it)
    w2_ref = next(it)
    if latent:
        ck_ref = next(it)
        cv_ref = next(it)
        cos_ref = next(it)
        sin_ref = next(it)
    xo_ref = next(it)
    if not latent:
        nk_ref = next(it)
        nv_ref = next(it)
    q_s = next(it)
    k_s = next(it)
    v_s = next(it)
    ub_s = next(it)
    vn_s = next(it)
    hp_s = next(it)
    zc_s = next(it)
    zs_s = next(it)
    mix_s = next(it)

    mod = mod_ref[0]
    sh1, sc1, gt1, sh2, sc2, gt2 = [
        mod[:, i * D_MODEL:(i + 1) * D_MODEL] for i in range(6)]
    g_attn = v1024_ref[0:1, :]
    g_mlp = v1024_ref[1:2, :]

    def vec(i):
        return v256_ref[i:i + 1, :]

    lane = lax.broadcasted_iota(jnp.int32, (1, GROUP_W), 1)
    row_i = lax.broadcasted_iota(jnp.int32, (GROUP_W, GROUP_W), 0)
    col_i = lax.broadcasted_iota(jnp.int32, (GROUP_W, GROUP_W), 1)
    gmat32 = jnp.where(row_i // A_QK_DIM == col_i // A_QK_DIM, 1.0, 0.0).astype(BF16)
    gmat64 = jnp.where(row_i // A_V_DIM == col_i // A_V_DIM, 1.0, 0.0).astype(BF16)

    lam_v = lam_ref[...]
    l1 = jnp.sum(lam_v[0:1] * lam_v[1:2], axis=-1, keepdims=True)
    l2 = jnp.sum(lam_v[2:3] * lam_v[3:4], axis=-1, keepdims=True)
    lam = jnp.exp(l1) - jnp.exp(l2) + lam_init

    scale = A_QK_DIM ** -0.5
    zero_pad = jnp.zeros((CONV_PAD, GROUP_W), F32)
    hp_s[0:CONV_PAD, :] = zero_pad
    hp_s[CONV_PAD + L:2 * CONV_PAD + L, :] = zero_pad
    if latent:
        k_s[0:past_len, :] = ck_ref[...].astype(BF16)
        v_s[0:past_len, :] = cv_ref[...].astype(BF16)
        swap_lo = (lane // (A_QK_DIM // 4)) % 2 == 0

    def phase1(r0):
        rows = pl.ds(r0, ROW_BLOCK)
        x = x_ref[0, rows, :]
        h = _rms(x, g_attn) * (1.0 + sc1) + sh1
        z = _dot(h.astype(BF16), w_in_ref[...])
        zq = z[:, 0:GROUP_W]
        zk = z[:, GROUP_W:2 * GROUP_W]
        zv = z[:, 2 * GROUP_W:3 * GROUP_W]
        zb = z[:, 3 * GROUP_W:5 * GROUP_W]
        zc = z[:, 5 * GROUP_W:7 * GROUP_W]
        zd = z[:, 7 * GROUP_W:8 * GROUP_W]

        q = _group_rms(zq, gmat32, A_QK_DIM, vec(_V_GQ))
        k = _group_rms(zk, gmat32, A_QK_DIM, vec(_V_GK))
        if latent:
            cs = cos_ref[rows, :]
            sn = sin_ref[rows, :]

            def rope(t):
                swapped = jnp.where(
                    swap_lo,
                    pltpu.roll(t, GROUP_W - A_QK_DIM // 4, axis=1),
                    pltpu.roll(t, A_QK_DIM // 4, axis=1))
                return t * cs + swapped * sn

            q = rope(q)
            k = rope(k)
        else:
            nk_ref[0, rows, :] = k
            nv_ref[0, rows, :] = zv
        q_s[rows, :] = q * scale
        k_s[pl.ds(past_len + r0, ROW_BLOCK), :] = k.astype(BF16)
        v_s[pl.ds(past_len + r0, ROW_BLOCK), :] = zv.astype(BF16)

        gb = jax.nn.gelu(zb, approximate=True)
        ub_s[rows, :] = gb[:, 0:GROUP_W]
        vn_s[rows, :] = _layer_norm(
            gb[:, GROUP_W:], vec(_V_GSG), vec(_V_BSG)).astype(BF16)

        hp_s[pl.ds(CONV_PAD + r0, ROW_BLOCK), :] = (
            zc[:, 0:GROUP_W] * _sigmoid(zc[:, GROUP_W:]))

        zcs = _dot(zd.astype(BF16), bdcs_ref[...])
        zc_s[rows, :] = zcs[:, 0:GROUP_W].astype(BF16)
        zs_s[rows, :] = zcs[:, GROUP_W:].astype(BF16)

    def phase2(r0):
        rows = pl.ds(r0, ROW_BLOCK)

        qf = q_s[rows, :]

        def head(hh, o):
            kb = k_s[...]
            vb = v_s[...]
            es = []
            for c in range(2):
                lo = hh * A_V_DIM + c * A_QK_DIM
                m = (lane >= lo) & (lane < lo + A_QK_DIM)
                qm = jnp.where(m, qf, 0.0).astype(BF16)
                s = lax.dot_general(qm, kb, (((1,), (1,)), ((), ())),
                                    preferred_element_type=F32)
                e = jnp.exp(s - jnp.max(s, axis=-1, keepdims=True))
                es.append((e, 1.0 / jnp.sum(e, axis=-1, keepdims=True)))
            w = es[0][0] * es[0][1] - es[1][0] * (lam * es[1][1])
            oh = _dot(w.astype(BF16), vb)
            hm = (lane >= hh * A_V_DIM) & (lane < (hh + 1) * A_V_DIM)
            return jnp.where(hm, oh, o)

        o = lax.fori_loop(0, A_HEADS, head, jnp.zeros((ROW_BLOCK, GROUP_W), F32))
        o_a = _group_rms(o, gmat64, A_V_DIM, vec(_V_GHEAD)) * (1.0 - lam_init)
        mix_s[rows, 0:GROUP_W] = o_a.astype(BF16)

        for c in range(ROW_BLOCK // CHUNK):
            crow = pl.ds(r0 + c * CHUNK, CHUNK)
            res = _dot(wsp_ref[...], vn_s[crow, :])
            mixed = bsp_ref[...]
            for g in range(B_GROUPS):
                gm = (lane >= g * (GROUP_W // B_GROUPS)) & (lane < (g + 1) * (GROUP_W // B_GROUPS))
                mixed = mixed + jnp.where(gm, res[g * CHUNK:(g + 1) * CHUNK, :], 0.0)
            mix_s[crow, GROUP_W:2 * GROUP_W] = (ub_s[crow, :] * mixed).astype(BF16)

        win = hp_s[pl.ds(r0, ROW_BLOCK + 2 * CONV_PAD), :]
        acc = jnp.zeros((ROW_BLOCK, GROUP_W), F32) + vec(_V_BDW)
        for s_off in range(8):
            part = None
            for m_off in range(0, 2 * CONV_PAD, 8):
                j = m_off + s_off - 1
                if j < 0 or j >= C_KERNEL:
                    continue
                term = win[m_off:m_off + ROW_BLOCK + 8, :] * wdw_ref[j:j + 1, :]
                part = term if part is None else part + term
            acc = acc + part[s_off:s_off + ROW_BLOCK, :]
        yc = _layer_norm(acc, vec(_V_GCONV), vec(_V_BCONV))
        mix_s[rows, 2 * GROUP_W:3 * GROUP_W] = (yc * _sigmoid(yc)).astype(BF16)

        o_d = (_dot(csl_ref[rows, 0:L], zc_s[...])
               + _dot(csl_ref[rows, L:2 * L], zs_s[...]))
        mix_s[rows, 3 * GROUP_W:4 * GROUP_W] = o_d.astype(BF16)

        x = x_ref[0, rows, :]
        x1 = x + gt1 * _dot(mix_s[rows, :], w_out_ref[...])
        h2 = (_rms(x1, g_mlp) * (1.0 + sc2) + sh2).astype(BF16)
        f = jnp.maximum(_dot(h2, w1_ref[...]), 0.0)
        f = (f * f).astype(BF16)
        xo_ref[0, rows, :] = x1 + gt2 * _dot(f, w2_ref[...])

    def for_each_block(phase):
        if n_blk == 1:
            phase(0)
        else:
            def step(r, carry):
                phase(pl.multiple_of(r * ROW_BLOCK, ROW_BLOCK))
                return carry
            lax.fori_loop(0, n_blk, step, 0)

    for_each_block(phase1)
    for_each_block(phase2)


def _const_spec(shape):
    nd = len(shape)
    return pl.BlockSpec(shape, lambda b, _n=nd: (0,) * _n, pipeline_mode=pl.Buffered(1))


def _trunk_layer(x, mod, layer, consts, weights, latent_inputs):
    n_seq, L, _ = x.shape
    latent = latent_inputs is not None
    past_len = latent_inputs[0].shape[2] if latent else 0
    lam_init = 0.8 - 0.6 * math.exp(-0.3 * layer)
    v1024, v256, lam4, wsp, bsp, wdw, bdcs, csl = consts
    w_in, w_out, w1, w2 = weights
    per_seq_mod = mod.shape[0] > 1

    x_mode = dict(pipeline_mode=pl.Buffered(1)) if latent else {}
    in_specs = [
        pl.BlockSpec((1, L, D_MODEL), lambda b: (b, 0, 0), **x_mode),
        pl.BlockSpec((1, 1, 6 * D_MODEL),
                     (lambda b: (b, 0, 0)) if per_seq_mod else (lambda b: (0, 0, 0))),
        _const_spec(v1024.shape), _const_spec(v256.shape), _const_spec(lam4.shape),
        _const_spec(wsp.shape), _const_spec(bsp.shape), _const_spec(wdw.shape),
        _const_spec(bdcs.shape), _const_spec(csl.shape),
        _const_spec(w_in.shape), _const_spec(w_out.shape),
        _const_spec(w1.shape), _const_spec(w2.shape),
    ]
    args = [x, mod, v1024, v256, lam4, wsp, bsp, wdw, bdcs, csl, w_in, w_out, w1, w2]
    out_specs = [pl.BlockSpec((1, L, D_MODEL), lambda b: (b, 0, 0), **x_mode)]
    out_shape = [jax.ShapeDtypeStruct(x.shape, F32)]
    if latent:
        cache_k, cache_v, cos_t, sin_t = latent_inputs
        kv_spec = pl.BlockSpec((None, None, past_len, GROUP_W),
                               lambda b, _l=layer: (b, _l, 0, 0))
        in_specs += [kv_spec, kv_spec, _const_spec(cos_t.shape), _const_spec(sin_t.shape)]
        args += [cache_k, cache_v, cos_t, sin_t]
    else:
        kv_out = pl.BlockSpec((1, L, GROUP_W), lambda b: (b, 0, 0))
        out_specs += [kv_out, kv_out]
        out_shape += [jax.ShapeDtypeStruct((n_seq, L, GROUP_W), F32)] * 2

    lk = past_len + L
    scratch = [
        pltpu.VMEM((L, GROUP_W), F32),
        pltpu.VMEM((lk, GROUP_W), BF16),
        pltpu.VMEM((lk, GROUP_W), BF16),
        pltpu.VMEM((L, GROUP_W), F32),
        pltpu.VMEM((L, GROUP_W), BF16),
        pltpu.VMEM((L + 2 * CONV_PAD, GROUP_W), F32),
        pltpu.VMEM((L, GROUP_W), BF16),
        pltpu.VMEM((L, GROUP_W), BF16),
        pltpu.VMEM((L, D_MODEL), BF16),
    ]
    kern = functools.partial(_layer_kernel, seq_len=L, past_len=past_len,
                             latent=latent, lam_init=lam_init)
    return pl.pallas_call(
        kern,
        grid=(n_seq,),
        in_specs=in_specs,
        out_specs=out_specs,
        out_shape=out_shape,
        scratch_shapes=scratch,
        compiler_params=pltpu.CompilerParams(
            dimension_semantics=("arbitrary",),
            vmem_limit_bytes=VMEM_LIMIT),
        name=("latent" if latent else "context") + f"_layer{layer}",
    )(*args)


def _dft_tables(L):
    nc = GROUP_W // D_GROUPS
    cc = np.arange(nc)
    ang_c = 2.0 * np.pi * ((cc[:, None] * cc[None, :]) % nc) / nc
    bd_c = np.kron(np.eye(D_GROUPS), np.cos(ang_c) / math.sqrt(nc))
    bd_s = np.kron(np.eye(D_GROUPS), np.sin(ang_c) / math.sqrt(nc))
    bdcs = np.concatenate([bd_c, bd_s], axis=1)
    pp = np.arange(L)
    ang_l = 2.0 * np.pi * ((pp[:, None] * pp[None, :]) % L) / L
    csl = np.concatenate([np.cos(ang_l), -np.sin(ang_l)], axis=1) / math.sqrt(L)
    return jnp.asarray(bdcs, F32).astype(BF16), jnp.asarray(csl, F32).astype(BF16)


def _rope_tables(n_tokens):
    rows = n_tokens // GRID_W
    row = jnp.repeat(jnp.arange(rows, dtype=F32), GRID_W)
    col = jnp.tile(jnp.arange(GRID_W, dtype=F32), rows)
    nf = A_QK_DIM // 4
    inv = ROPE_BASE ** (-jnp.arange(nf, dtype=F32) / nf)
    ang = jnp.stack([row[:, None] * inv, col[:, None] * inv], axis=1)
    cos, sin = jnp.cos(ang), jnp.sin(ang)
    cos32 = jnp.stack([cos, cos], axis=2).reshape(n_tokens, A_QK_DIM)
    sin32 = jnp.stack([-sin, sin], axis=2).reshape(n_tokens, A_QK_DIM)
    reps = GROUP_W // A_QK_DIM
    return jnp.tile(cos32, (1, reps)), jnp.tile(sin32, (1, reps))


def kernel(x_prompt, x_sample, c, cache_k, cache_v, c_ctx, w_ada, b_ada, g_attn_norm, g_mlp_norm, w_in, g_q, g_k, lam_q1, lam_k1, lam_q2, lam_k2, g_head, g_sg, b_sg, w_spatial, b_spatial, w_dw, b_dw, g_conv, b_conv, w_out, w_ff1, w_ff2):
    n_p, l_p, _ = x_prompt.shape
    n_s, l_s, _ = x_sample.shape
    past_len = cache_k.shape[2]

    cond = jnp.concatenate(
        [c_ctx[None, :], c, jnp.zeros((ADA_ROWS - 1 - n_s, D_MODEL), F32)], axis=0)
    mod_all = _ada_modulation(cond, w_ada, b_ada)

    w_in_b = w_in.astype(BF16)
    w_out_b = w_out.astype(BF16)
    w1_b = w_ff1.astype(BF16)
    w2_b = w_ff2.astype(BF16)

    bdcs, csl_p = _dft_tables(l_p)
    _, csl_s = _dft_tables(l_s)
    cos_t, sin_t = _rope_tables(l_s)
    ck4 = cache_k.reshape(n_s, DEPTH, past_len, GROUP_W)
    cv4 = cache_v.reshape(n_s, DEPTH, past_len, GROUP_W)

    xp, xs = x_prompt, x_sample
    k_list, v_list = [], []
    for l in range(DEPTH):
        v1024 = jnp.stack([g_attn_norm[l], g_mlp_norm[l]], axis=0)
        rows = [jnp.tile(g_q[l], GROUP_W // A_QK_DIM), jnp.tile(g_k[l], GROUP_W // A_QK_DIM),
                jnp.tile(g_head[l], GROUP_W // A_V_DIM), g_sg[l], b_sg[l], b_dw[l],
                g_conv[l], b_conv[l]]
        v256 = jnp.concatenate(
            [jnp.stack(rows, axis=0), jnp.zeros((16 - len(rows), GROUP_W), F32)], axis=0)
        lam4 = jnp.stack([lam_q1[l], lam_k1[l], lam_q2[l], lam_k2[l]], axis=0)
        wsp = w_spatial[l].reshape(B_GROUPS * CHUNK, CHUNK).astype(BF16)
        bsp = jnp.repeat(b_spatial[l].T, GROUP_W // B_GROUPS, axis=1)
        wdw = jnp.concatenate([w_dw[l], jnp.zeros((1, GROUP_W), F32)], axis=0)
        weights = (w_in_b[l], w_out_b[l], w1_b[l], w2_b[l])

        consts_p = (v1024, v256, lam4, wsp, bsp, wdw, bdcs, csl_p)
        xp, k_l, v_l = _trunk_layer(xp, mod_all[l, 0:1][:, None, :], l, consts_p, weights, None)
        k_list.append(k_l)
        v_list.append(v_l)

        consts_s = (v1024, v256, lam4, wsp, bsp, wdw, bdcs, csl_s)
        xs = _trunk_layer(xs, mod_all[l, 1:1 + n_s][:, None, :], l, consts_s, weights,
                          (ck4, cv4, cos_t, sin_t))[0]

    new_k = jnp.stack(k_list, axis=1).reshape(n_p, DEPTH, l_p, A_HEADS, 2 * A_QK_DIM)
    new_v = jnp.stack(v_list, axis=1).reshape(n_p, DEPTH, l_p, A_HEADS, A_V_DIM)
    return (xp, xs, new_k, new_v)
```

```python
import functools
import math

import numpy as np
import jax
import jax.numpy as jnp
from jax import lax
from jax.experimental import pallas as pl
from jax.experimental.pallas import tpu as pltpu

F32 = jnp.float32
BF16 = jnp.bfloat16

D_MODEL = 1024
DEPTH = 4
GRID_W = 64
GROUP_W = 256
A_HEADS = 4
A_V_DIM = 64
A_QK_DIM = 32
B_GROUPS = 4
CHUNK = 128
C_KERNEL = 31
D_GROUPS = 4
D_FF = 4 * D_MODEL
ROPE_BASE = 10000.0
EPS = 1e-6
IN_W = 8 * GROUP_W

ROW_BLOCK = 256
CONV_PAD = 16
ADA_ROWS = 8
ADA_COLS = 2048
VMEM_LIMIT = 58 * 1024 * 1024


def _dot(a, b):
    return jnp.dot(a, b, preferred_element_type=F32)


def _sigmoid(x):
    return 1.0 / (1.0 + jnp.exp(-x))


def _rms(x, g):
    ms = jnp.mean(x * x, axis=-1, keepdims=True)
    return x * lax.rsqrt(ms + EPS) * g


def _layer_norm(x, g, b):
    mu = jnp.mean(x, axis=-1, keepdims=True)
    xc = x - mu
    var = jnp.mean(xc * xc, axis=-1, keepdims=True)
    return xc * lax.rsqrt(var + EPS) * g + b


def _group_rms(x, gmat, gsize, g):
    ms = _dot((x * x).astype(BF16), gmat) * (1.0 / gsize)
    return x * lax.rsqrt(ms + EPS) * g


def _ada_kernel(cond_ref, w_ref, b_ref, o_ref):
    cnd = cond_ref[...]
    s = (cnd * _sigmoid(cnd)).astype(BF16)
    o_ref[0] = _dot(s, w_ref[0].astype(BF16)) + b_ref[0]


def _ada_modulation(cond, w_ada, b_ada):
    ncol = w_ada.shape[-1]
    return pl.pallas_call(
        _ada_kernel,
        grid=(DEPTH, ncol // ADA_COLS),
        in_specs=[
            pl.BlockSpec((ADA_ROWS, D_MODEL), lambda l, j: (0, 0)),
            pl.BlockSpec((1, D_MODEL, ADA_COLS), lambda l, j: (l, 0, j)),
            pl.BlockSpec((1, 1, ADA_COLS), lambda l, j: (l, 0, j)),
        ],
        out_specs=pl.BlockSpec((1, ADA_ROWS, ADA_COLS), lambda l, j: (l, 0, j)),
        out_shape=jax.ShapeDtypeStruct((DEPTH, ADA_ROWS, ncol), F32),
        compiler_params=pltpu.CompilerParams(
            dimension_semantics=("arbitrary", "arbitrary"),
            vmem_limit_bytes=VMEM_LIMIT),
        name="ada_modulation",
    )(cond, w_ada, b_ada.reshape(DEPTH, 1, ncol))


_V_GQ, _V_GK, _V_GHEAD, _V_GSG, _V_BSG, _V_BDW, _V_GCONV, _V_BCONV = range(8)


def _layer_kernel(*refs, seq_len, past_len, latent):
    L = seq_len
    n_blk = L // ROW_BLOCK
    it = iter(refs)
    x_ref = next(it)
    mod_ref = next(it)
    v1024_ref = next(it)
    v256_ref = next(it)
    lam_ref = next(it)
    wsp_ref = next(it)
    bsp_ref = next(it)
    wdw_ref = next(it)
    bdcs_ref = next(it)
    csl_ref = next(it)
    w_in_ref = next(it)
    w_out_ref = next(it)
    w1_ref = next(it)
    w2_ref = next(it)
    if latent:
        ck_ref = next(it)
        cv_ref = next(it)
        cos_ref = next(it)
        sin_ref = next(it)
    xo_ref = next(it)
    if not latent:
        nk_ref = next(it)
        nv_ref = next(it)
    q_s = next(it)
    k_s = next(it)
    v_s = next(it)
    ub_s = next(it)
    vn_s = next(it)
    hp_s = next(it)
    zc_s = next(it)
    zs_s = next(it)
    mix_s = next(it)

    mod_row = 1 + pl.program_id(1) if latent else 0
    mod = mod_ref[pl.ds(mod_row, 1), :]
    sh1, sc1, gt1, sh2, sc2, gt2 = [
        mod[:, i * D_MODEL:(i + 1) * D_MODEL] for i in range(6)]
    g_attn = v1024_ref[0:1, :]
    g_mlp = v1024_ref[1:2, :]

    def vec(i):
        return v256_ref[i:i + 1, :]

    lane = lax.broadcasted_iota(jnp.int32, (1, GROUP_W), 1)
    row_i = lax.broadcasted_iota(jnp.int32, (GROUP_W, GROUP_W), 0)
    col_i = lax.broadcasted_iota(jnp.int32, (GROUP_W, GROUP_W), 1)
    gmat32 = jnp.where(row_i // A_QK_DIM == col_i // A_QK_DIM, 1.0, 0.0).astype(BF16)
    gmat64 = jnp.where(row_i // A_V_DIM == col_i // A_V_DIM, 1.0, 0.0).astype(BF16)

    lam_v = lam_ref[...]
    l1 = jnp.sum(lam_v[0:1] * lam_v[1:2], axis=-1, keepdims=True)
    l2 = jnp.sum(lam_v[2:3] * lam_v[3:4], axis=-1, keepdims=True)
    lam_init = lam_v[4:5, 0:1]
    lam = jnp.exp(l1) - jnp.exp(l2) + lam_init

    scale = A_QK_DIM ** -0.5
    zero_pad = jnp.zeros((CONV_PAD, GROUP_W), F32)
    hp_s[0:CONV_PAD, :] = zero_pad
    hp_s[CONV_PAD + L:2 * CONV_PAD + L, :] = zero_pad
    if latent:
        k_s[0:past_len, :] = ck_ref[...].astype(BF16)
        v_s[0:past_len, :] = cv_ref[...].astype(BF16)
        swap_lo = (lane // (A_QK_DIM // 4)) % 2 == 0

    def phase1(r0):
        rows = pl.ds(r0, ROW_BLOCK)
        x = x_ref[0, rows, :]
        h = _rms(x, g_attn) * (1.0 + sc1) + sh1
        z = _dot(h.astype(BF16), w_in_ref[...])
        zq = z[:, 0:GROUP_W]
        zk = z[:, GROUP_W:2 * GROUP_W]
        zv = z[:, 2 * GROUP_W:3 * GROUP_W]
        zb = z[:, 3 * GROUP_W:5 * GROUP_W]
        zc = z[:, 5 * GROUP_W:7 * GROUP_W]
        zd = z[:, 7 * GROUP_W:8 * GROUP_W]

        q = _group_rms(zq, gmat32, A_QK_DIM, vec(_V_GQ))
        k = _group_rms(zk, gmat32, A_QK_DIM, vec(_V_GK))
        if latent:
            cs = cos_ref[rows, :]
            sn = sin_ref[rows, :]

            def rope(t):
                swapped = jnp.where(
                    swap_lo,
                    pltpu.roll(t, GROUP_W - A_QK_DIM // 4, axis=1),
                    pltpu.roll(t, A_QK_DIM // 4, axis=1))
                return t * cs + swapped * sn

            q = rope(q)
            k = rope(k)
        else:
            nk_ref[rows, :] = k
            nv_ref[rows, :] = zv
        q_s[rows, :] = q * scale
        k_s[pl.ds(past_len + r0, ROW_BLOCK), :] = k.astype(BF16)
        v_s[pl.ds(past_len + r0, ROW_BLOCK), :] = zv.astype(BF16)

        gb = jax.nn.gelu(zb, approximate=True)
        ub_s[rows, :] = gb[:, 0:GROUP_W]
        vn_s[rows, :] = _layer_norm(
            gb[:, GROUP_W:], vec(_V_GSG), vec(_V_BSG)).astype(BF16)

        hp_s[pl.ds(CONV_PAD + r0, ROW_BLOCK), :] = (
            zc[:, 0:GROUP_W] * _sigmoid(zc[:, GROUP_W:]))

        zcs = _dot(zd.astype(BF16), bdcs_ref[...])
        zc_s[rows, :] = zcs[:, 0:GROUP_W].astype(BF16)
        zs_s[rows, :] = zcs[:, GROUP_W:].astype(BF16)

    def phase2(r0):
        rows = pl.ds(r0, ROW_BLOCK)

        qf = q_s[rows, :]

        def head(hh, o):
            kb = k_s[...]
            vb = v_s[...]
            es = []
            for c in range(2):
                lo = hh * A_V_DIM + c * A_QK_DIM
                m = (lane >= lo) & (lane < lo + A_QK_DIM)
                qm = jnp.where(m, qf, 0.0).astype(BF16)
                s = lax.dot_general(qm, kb, (((1,), (1,)), ((), ())),
                                    preferred_element_type=F32)
                e = jnp.exp(s - jnp.max(s, axis=-1, keepdims=True))
                es.append((e, 1.0 / jnp.sum(e, axis=-1, keepdims=True)))
            w = es[0][0] * es[0][1] - es[1][0] * (lam * es[1][1])
            oh = _dot(w.astype(BF16), vb)
            hm = (lane >= hh * A_V_DIM) & (lane < (hh + 1) * A_V_DIM)
            return jnp.where(hm, oh, o)

        o = lax.fori_loop(0, A_HEADS, head, jnp.zeros((ROW_BLOCK, GROUP_W), F32),
                          unroll=not latent)
        o_a = _group_rms(o, gmat64, A_V_DIM, vec(_V_GHEAD)) * (1.0 - lam_init)
        mix_s[rows, 0:GROUP_W] = o_a.astype(BF16)

        for c in range(ROW_BLOCK // CHUNK):
            crow = pl.ds(r0 + c * CHUNK, CHUNK)
            res = _dot(wsp_ref[...], vn_s[crow, :])
            mixed = bsp_ref[...]
            for g in range(B_GROUPS):
                gm = (lane >= g * (GROUP_W // B_GROUPS)) & (lane < (g + 1) * (GROUP_W // B_GROUPS))
                mixed = mixed + jnp.where(gm, res[g * CHUNK:(g + 1) * CHUNK, :], 0.0)
            mix_s[crow, GROUP_W:2 * GROUP_W] = (ub_s[crow, :] * mixed).astype(BF16)

        win = hp_s[pl.ds(r0, ROW_BLOCK + 2 * CONV_PAD), :]
        acc = jnp.zeros((ROW_BLOCK, GROUP_W), F32) + vec(_V_BDW)
        for s_off in range(8):
            part = None
            for m_off in range(0, 2 * CONV_PAD, 8):
                j = m_off + s_off - 1
                if j < 0 or j >= C_KERNEL:
                    continue
                term = win[m_off:m_off + ROW_BLOCK + 8, :] * wdw_ref[j:j + 1, :]
                part = term if part is None else part + term
            acc = acc + part[s_off:s_off + ROW_BLOCK, :]
        yc = _layer_norm(acc, vec(_V_GCONV), vec(_V_BCONV))
        mix_s[rows, 2 * GROUP_W:3 * GROUP_W] = (yc * _sigmoid(yc)).astype(BF16)

        o_d = (_dot(csl_ref[rows, 0:L], zc_s[...])
               + _dot(csl_ref[rows, L:2 * L], zs_s[...]))
        mix_s[rows, 3 * GROUP_W:4 * GROUP_W] = o_d.astype(BF16)

        x = x_ref[0, rows, :]
        x1 = x + gt1 * _dot(mix_s[rows, :], w_out_ref[...])
        h2 = (_rms(x1, g_mlp) * (1.0 + sc2) + sh2).astype(BF16)
        f = jnp.maximum(_dot(h2, w1_ref[...]), 0.0)
        f = (f * f).astype(BF16)
        xo_ref[0, rows, :] = x1 + gt2 * _dot(f, w2_ref[...])

    def for_each_block(phase):
        if n_blk == 1:
            phase(0)
        else:
            def step(r, carry):
                phase(pl.multiple_of(r * ROW_BLOCK, ROW_BLOCK))
                return carry
            lax.fori_loop(0, n_blk, step, 0)

    for_each_block(phase1)
    for_each_block(phase2)


def _trunk_layers(x, layer0, n_layers, mod_all, params, consts, weights, latent_inputs):
    n_seq, L, _ = x.shape
    latent = latent_inputs is not None
    past_len = latent_inputs[0].shape[2] if latent else 0
    bdcs, csl = consts

    def layer_spec(arr, **kw):
        nd = arr.ndim - 1
        return pl.BlockSpec((None,) + arr.shape[1:],
                            lambda l, b, _n=nd: (layer0 + l,) + (0,) * _n, **kw)

    def const_spec(arr):
        return pl.BlockSpec(arr.shape, lambda l, b, _n=arr.ndim: (0,) * _n,
                            pipeline_mode=pl.Buffered(1))

    x_mode = dict(pipeline_mode=pl.Buffered(1)) if latent else {}
    w_mode = dict(pipeline_mode=pl.Buffered(1)) if n_layers == 1 else {}
    x_spec = pl.BlockSpec((1, L, D_MODEL), lambda l, b: (b, 0, 0), **x_mode)
    in_specs = ([x_spec, layer_spec(mod_all)]
                + [layer_spec(p) for p in params]
                + [const_spec(bdcs), const_spec(csl)]
                + [layer_spec(w, **w_mode) for w in weights])
    args = [x, mod_all, *params, bdcs, csl, *weights]
    out_specs = [x_spec]
    out_shape = [jax.ShapeDtypeStruct(x.shape, F32)]
    if latent:
        cache_k, cache_v, cos_t, sin_t = latent_inputs
        kv_spec = pl.BlockSpec((None, None, past_len, GROUP_W),
                               lambda l, b: (b, layer0 + l, 0, 0))
        in_specs += [kv_spec, kv_spec, const_spec(cos_t), const_spec(sin_t)]
        args += [cache_k, cache_v, cos_t, sin_t]
    else:
        kv_out = pl.BlockSpec((None, None, L, GROUP_W), lambda l, b: (b, layer0 + l, 0, 0))
        out_specs += [kv_out, kv_out]
        out_shape += [jax.ShapeDtypeStruct((n_seq, DEPTH, L, GROUP_W), F32)] * 2

    lk = past_len + L
    scratch = [
        pltpu.VMEM((L, GROUP_W), F32),
        pltpu.VMEM((lk, GROUP_W), BF16),
        pltpu.VMEM((lk, GROUP_W), BF16),
        pltpu.VMEM((L, GROUP_W), F32),
        pltpu.VMEM((L, GROUP_W), BF16),
        pltpu.VMEM((L + 2 * CONV_PAD, GROUP_W), F32),
        pltpu.VMEM((L, GROUP_W), BF16),
        pltpu.VMEM((L, GROUP_W), BF16),
        pltpu.VMEM((L, D_MODEL), BF16),
    ]
    if not latent:
        assert layer0 == 0 and n_layers == DEPTH
    kern = functools.partial(_layer_kernel, seq_len=L, past_len=past_len, latent=latent)
    return pl.pallas_call(
        kern,
        grid=(n_layers, n_seq),
        in_specs=in_specs,
        out_specs=out_specs,
        out_shape=out_shape,
        scratch_shapes=scratch,
        input_output_aliases={0: 0} if n_layers > 1 else {},
        compiler_params=pltpu.CompilerParams(
            dimension_semantics=("arbitrary", "arbitrary"),
            vmem_limit_bytes=VMEM_LIMIT),
        name=("latent" if latent else "context") + f"_layers{layer0}_{layer0 + n_layers}",
    )(*args)


def _dft_tables(L):
    nc = GROUP_W // D_GROUPS
    cc = np.arange(nc)
    ang_c = 2.0 * np.pi * ((cc[:, None] * cc[None, :]) % nc) / nc
    bd_c = np.kron(np.eye(D_GROUPS), np.cos(ang_c) / math.sqrt(nc))
    bd_s = np.kron(np.eye(D_GROUPS), np.sin(ang_c) / math.sqrt(nc))
    bdcs = np.concatenate([bd_c, bd_s], axis=1)
    pp = np.arange(L)
    ang_l = 2.0 * np.pi * ((pp[:, None] * pp[None, :]) % L) / L
    csl = np.concatenate([np.cos(ang_l), -np.sin(ang_l)], axis=1) / math.sqrt(L)
    return jnp.asarray(bdcs, F32).astype(BF16), jnp.asarray(csl, F32).astype(BF16)


def _rope_tables(n_tokens):
    rows = n_tokens // GRID_W
    row = jnp.repeat(jnp.arange(rows, dtype=F32), GRID_W)
    col = jnp.tile(jnp.arange(GRID_W, dtype=F32), rows)
    nf = A_QK_DIM // 4
    inv = ROPE_BASE ** (-jnp.arange(nf, dtype=F32) / nf)
    ang = jnp.stack([row[:, None] * inv, col[:, None] * inv], axis=1)
    cos, sin = jnp.cos(ang), jnp.sin(ang)
    cos32 = jnp.stack([cos, cos], axis=2).reshape(n_tokens, A_QK_DIM)
    sin32 = jnp.stack([-sin, sin], axis=2).reshape(n_tokens, A_QK_DIM)
    reps = GROUP_W // A_QK_DIM
    return jnp.tile(cos32, (1, reps)), jnp.tile(sin32, (1, reps))


def kernel(x_prompt, x_sample, c, cache_k, cache_v, c_ctx, w_ada, b_ada, g_attn_norm, g_mlp_norm, w_in, g_q, g_k, lam_q1, lam_k1, lam_q2, lam_k2, g_head, g_sg, b_sg, w_spatial, b_spatial, w_dw, b_dw, g_conv, b_conv, w_out, w_ff1, w_ff2):
    n_p, l_p, _ = x_prompt.shape
    n_s, l_s, _ = x_sample.shape
    past_len = cache_k.shape[2]

    cond = jnp.concatenate(
        [c_ctx[None, :], c, jnp.zeros((ADA_ROWS - 1 - n_s, D_MODEL), F32)], axis=0)
    mod_all = _ada_modulation(cond, w_ada, b_ada)

    w_in_b = w_in.astype(BF16)
    w_out_b = w_out.astype(BF16)
    w1_b = w_ff1.astype(BF16)
    w2_b = w_ff2.astype(BF16)

    bdcs, csl_p = _dft_tables(l_p)
    _, csl_s = _dft_tables(l_s)
    cos_t, sin_t = _rope_tables(l_s)
    ck4 = cache_k.reshape(n_s, DEPTH, past_len, GROUP_W)
    cv4 = cache_v.reshape(n_s, DEPTH, past_len, GROUP_W)

    v1024 = jnp.stack([g_attn_norm, g_mlp_norm], axis=1)
    rows = [jnp.tile(g_q, (1, GROUP_W // A_QK_DIM)), jnp.tile(g_k, (1, GROUP_W // A_QK_DIM)),
            jnp.tile(g_head, (1, GROUP_W // A_V_DIM)), g_sg, b_sg, b_dw, g_conv, b_conv]
    v256 = jnp.concatenate(
        [jnp.stack(rows, axis=1), jnp.zeros((DEPTH, 16 - len(rows), GROUP_W), F32)], axis=1)
    lam_init = jnp.asarray(
        [0.8 - 0.6 * math.exp(-0.3 * l) for l in range(DEPTH)], F32)
    lam5 = jnp.stack([lam_q1, lam_k1, lam_q2, lam_k2,
                      jnp.broadcast_to(lam_init[:, None], lam_q1.shape)], axis=1)
    lam5 = jnp.concatenate([lam5, jnp.zeros((DEPTH, 3, A_QK_DIM), F32)], axis=1)
    wsp = w_spatial.reshape(DEPTH, B_GROUPS * CHUNK, CHUNK).astype(BF16)
    bsp = jnp.repeat(jnp.swapaxes(b_spatial, 1, 2), GROUP_W // B_GROUPS, axis=2)
    wdw = jnp.concatenate([w_dw, jnp.zeros((DEPTH, 1, GROUP_W), F32)], axis=1)
    params = (v1024, v256, lam5, wsp, bsp, wdw)
    weights = (w_in_b, w_out_b, w1_b, w2_b)

    xp, new_k, new_v = _trunk_layers(
        x_prompt, 0, DEPTH, mod_all, params, (bdcs, csl_p), weights, None)
    xs = x_sample
    for l in range(DEPTH):
        xs = _trunk_layers(xs, l, 1, mod_all, params, (bdcs, csl_s), weights,
                           (ck4, cv4, cos_t, sin_t))[0]

    new_k = new_k.reshape(n_p, DEPTH, l_p, A_HEADS, 2 * A_QK_DIM)
    new_v = new_v.reshape(n_p, DEPTH, l_p, A_HEADS, A_V_DIM)
    return (xp, xs, new_k, new_v)
```

```python
import functools
import math

import numpy as np
import jax
import jax.numpy as jnp
from jax import lax
from jax.experimental import pallas as pl
from jax.experimental.pallas import tpu as pltpu

F32 = jnp.float32
BF16 = jnp.bfloat16

D_MODEL = 1024
DEPTH = 4
GRID_W = 64
GROUP_W = 256
A_HEADS = 4
A_V_DIM = 64
A_QK_DIM = 32
B_GROUPS = 4
CHUNK = 128
C_KERNEL = 31
D_GROUPS = 4
D_FF = 4 * D_MODEL
ROPE_BASE = 10000.0
EPS = 1e-6
IN_W = 8 * GROUP_W

ROW_BLOCK = 256
CONV_PAD = 16
ADA_ROWS = 8
ADA_COLS = 2048
CTX_SEQS_PER_STEP = 1
VMEM_LIMIT = 58 * 1024 * 1024


def _dot(a, b):
    return jnp.dot(a, b, preferred_element_type=F32)


def _sigmoid(x):
    return 1.0 / (1.0 + jnp.exp(-x))


def _rms(x, g):
    ms = jnp.mean(x * x, axis=-1, keepdims=True)
    return x * lax.rsqrt(ms + EPS) * g


def _layer_norm(x, g, b):
    mu = jnp.mean(x, axis=-1, keepdims=True)
    xc = x - mu
    var = jnp.mean(xc * xc, axis=-1, keepdims=True)
    return xc * lax.rsqrt(var + EPS) * g + b


def _group_rms(x, gmat, gsize, g):
    ms = _dot((x * x).astype(BF16), gmat) * (1.0 / gsize)
    return x * lax.rsqrt(ms + EPS) * g


def _ada_kernel(cond_ref, w_ref, b_ref, o_ref):
    cnd = cond_ref[...]
    s = (cnd * _sigmoid(cnd)).astype(BF16)
    o_ref[0] = _dot(s, w_ref[0].astype(BF16)) + b_ref[0]


def _ada_modulation(cond, w_ada, b_ada):
    ncol = w_ada.shape[-1]
    return pl.pallas_call(
        _ada_kernel,
        grid=(DEPTH, ncol // ADA_COLS),
        in_specs=[
            pl.BlockSpec((ADA_ROWS, D_MODEL), lambda l, j: (0, 0)),
            pl.BlockSpec((1, D_MODEL, ADA_COLS), lambda l, j: (l, 0, j)),
            pl.BlockSpec((1, 1, ADA_COLS), lambda l, j: (l, 0, j)),
        ],
        out_specs=pl.BlockSpec((1, ADA_ROWS, ADA_COLS), lambda l, j: (l, 0, j)),
        out_shape=jax.ShapeDtypeStruct((DEPTH, ADA_ROWS, ncol), F32),
        compiler_params=pltpu.CompilerParams(
            dimension_semantics=("arbitrary", "arbitrary"),
            vmem_limit_bytes=VMEM_LIMIT),
        name="ada_modulation",
    )(cond, w_ada, b_ada.reshape(DEPTH, 1, ncol))


_V_GQ, _V_GK, _V_GHEAD, _V_GSG, _V_BSG, _V_BDW, _V_GCONV, _V_BCONV = range(8)


def _layer_kernel(*refs, seq_len, past_len, latent, n_sub, n_alias_in):
    L = seq_len
    n_blk = L // ROW_BLOCK
    n_in = 14 + (4 if latent else 0)
    refs = refs[:n_in] + refs[n_in + n_alias_in:]
    it = iter(refs)
    x_all = next(it)
    mod_ref = next(it)
    v1024_ref = next(it)
    v256_ref = next(it)
    lam_ref = next(it)
    wsp_ref = next(it)
    bsp_ref = next(it)
    wdw_ref = next(it)
    bdcs_ref = next(it)
    csl_ref = next(it)
    w_in_ref = next(it)
    w_out_ref = next(it)
    w1_ref = next(it)
    w2_ref = next(it)
    if latent:
        ck_ref = next(it)
        cv_ref = next(it)
        cos_ref = next(it)
        sin_ref = next(it)
    xo_all = next(it)
    if not latent:
        nk_all = next(it)
        nv_all = next(it)
    q_all = next(it)
    k_all = next(it)
    v_all = next(it)
    ub_all = next(it)
    vn_all = next(it)
    hp_all = next(it)
    zc_all = next(it)
    zs_all = next(it)
    mix_all = next(it)

    mod_row = 1 + pl.program_id(1) if latent else 0
    mod = mod_ref[pl.ds(mod_row, 1), :]
    sh1, sc1, gt1, sh2, sc2, gt2 = [
        mod[:, i * D_MODEL:(i + 1) * D_MODEL] for i in range(6)]
    g_attn = v1024_ref[0:1, :]
    g_mlp = v1024_ref[1:2, :]

    def vec(i):
        return v256_ref[i:i + 1, :]

    lane = lax.broadcasted_iota(jnp.int32, (1, GROUP_W), 1)
    row_i = lax.broadcasted_iota(jnp.int32, (GROUP_W, GROUP_W), 0)
    col_i = lax.broadcasted_iota(jnp.int32, (GROUP_W, GROUP_W), 1)
    gmat32 = jnp.where(row_i // A_QK_DIM == col_i // A_QK_DIM, 1.0, 0.0).astype(BF16)
    gmat64 = jnp.where(row_i // A_V_DIM == col_i // A_V_DIM, 1.0, 0.0).astype(BF16)

    lam_v = lam_ref[...]
    l1 = jnp.sum(lam_v[0:1] * lam_v[1:2], axis=-1, keepdims=True)
    l2 = jnp.sum(lam_v[2:3] * lam_v[3:4], axis=-1, keepdims=True)
    lam_init = lam_v[4:5, 0:1]
    lam = jnp.exp(l1) - jnp.exp(l2) + lam_init

    scale = A_QK_DIM ** -0.5
    zero_pad = jnp.zeros((CONV_PAD, GROUP_W), F32)
    if latent:
        swap_lo = (lane // (A_QK_DIM // 4)) % 2 == 0

    def for_each_block(phase):
        if n_blk == 1:
            phase(0)
        else:
            def step(r, carry):
                phase(pl.multiple_of(r * ROW_BLOCK, ROW_BLOCK))
                return carry
            lax.fori_loop(0, n_blk, step, 0)

    def run_sequence(si):
        x_ref = x_all.at[si]
        xo_ref = xo_all.at[si]
        q_s = q_all.at[si]
        k_s = k_all.at[si]
        v_s = v_all.at[si]
        ub_s = ub_all.at[si]
        vn_s = vn_all.at[si]
        hp_s = hp_all.at[si]
        zc_s = zc_all.at[si]
        zs_s = zs_all.at[si]
        mix_s = mix_all.at[si]

        hp_s[0:CONV_PAD, :] = zero_pad
        hp_s[CONV_PAD + L:2 * CONV_PAD + L, :] = zero_pad
        if latent:
            k_s[0:past_len, :] = ck_ref[...].astype(BF16)
            v_s[0:past_len, :] = cv_ref[...].astype(BF16)

        def phase1(r0):
            rows = pl.ds(r0, ROW_BLOCK)
            x = x_ref[rows, :]
            h = _rms(x, g_attn) * (1.0 + sc1) + sh1
            z = _dot(h.astype(BF16), w_in_ref[...])
            zq = z[:, 0:GROUP_W]
            zk = z[:, GROUP_W:2 * GROUP_W]
            zv = z[:, 2 * GROUP_W:3 * GROUP_W]
            zb = z[:, 3 * GROUP_W:5 * GROUP_W]
            zc = z[:, 5 * GROUP_W:7 * GROUP_W]
            zd = z[:, 7 * GROUP_W:8 * GROUP_W]

            q = _group_rms(zq, gmat32, A_QK_DIM, vec(_V_GQ))
            k = _group_rms(zk, gmat32, A_QK_DIM, vec(_V_GK))
            if latent:
                cs = cos_ref[rows, :]
                sn = sin_ref[rows, :]

                def rope(t):
                    swapped = jnp.where(
                        swap_lo,
                        pltpu.roll(t, GROUP_W - A_QK_DIM // 4, axis=1),
                        pltpu.roll(t, A_QK_DIM // 4, axis=1))
                    return t * cs + swapped * sn

                q = rope(q)
                k = rope(k)
            else:
                nk_all[si, rows, :] = k
                nv_all[si, rows, :] = zv
            q_s[rows, :] = q * scale
            k_s[pl.ds(past_len + r0, ROW_BLOCK), :] = k.astype(BF16)
            v_s[pl.ds(past_len + r0, ROW_BLOCK), :] = zv.astype(BF16)

            gb = jax.nn.gelu(zb, approximate=True)
            ub_s[rows, :] = gb[:, 0:GROUP_W]
            vn_s[rows, :] = _layer_norm(
                gb[:, GROUP_W:], vec(_V_GSG), vec(_V_BSG)).astype(BF16)

            hp_s[pl.ds(CONV_PAD + r0, ROW_BLOCK), :] = (
                zc[:, 0:GROUP_W] * _sigmoid(zc[:, GROUP_W:]))

            zcs = _dot(zd.astype(BF16), bdcs_ref[...])
            zc_s[rows, :] = zcs[:, 0:GROUP_W].astype(BF16)
            zs_s[rows, :] = zcs[:, GROUP_W:].astype(BF16)

        def phase2(r0):
            rows = pl.ds(r0, ROW_BLOCK)

            qf = q_s[rows, :]

            def head(hh, o):
                kb = k_s[...]
                vb = v_s[...]
                es = []
                for c in range(2):
                    lo = hh * A_V_DIM + c * A_QK_DIM
                    m = (lane >= lo) & (lane < lo + A_QK_DIM)
                    qm = jnp.where(m, qf, 0.0).astype(BF16)
                    s = lax.dot_general(qm, kb, (((1,), (1,)), ((), ())),
                                        preferred_element_type=F32)
                    e = jnp.exp(s - jnp.max(s, axis=-1, keepdims=True))
                    es.append((e, 1.0 / jnp.sum(e, axis=-1, keepdims=True)))
                w = es[0][0] * es[0][1] - es[1][0] * (lam * es[1][1])
                oh = _dot(w.astype(BF16), vb)
                hm = (lane >= hh * A_V_DIM) & (lane < (hh + 1) * A_V_DIM)
                return jnp.where(hm, oh, o)

            o = lax.fori_loop(0, A_HEADS, head, jnp.zeros((ROW_BLOCK, GROUP_W), F32),
                              unroll=not latent)
            o_a = _group_rms(o, gmat64, A_V_DIM, vec(_V_GHEAD)) * (1.0 - lam_init)
            mix_s[rows, 0:GROUP_W] = o_a.astype(BF16)

            for c in range(ROW_BLOCK // CHUNK):
                crow = pl.ds(r0 + c * CHUNK, CHUNK)
                res = _dot(wsp_ref[...], vn_s[crow, :])
                mixed = bsp_ref[...]
                for g in range(B_GROUPS):
                    gm = ((lane >= g * (GROUP_W // B_GROUPS))
                          & (lane < (g + 1) * (GROUP_W // B_GROUPS)))
                    mixed = mixed + jnp.where(gm, res[g * CHUNK:(g + 1) * CHUNK, :], 0.0)
                mix_s[crow, GROUP_W:2 * GROUP_W] = (ub_s[crow, :] * mixed).astype(BF16)

            win = hp_s[pl.ds(r0, ROW_BLOCK + 2 * CONV_PAD), :]
            acc = jnp.zeros((ROW_BLOCK, GROUP_W), F32) + vec(_V_BDW)
            for s_off in range(8):
                part = None
                for m_off in range(0, 2 * CONV_PAD, 8):
                    j = m_off + s_off - 1
                    if j < 0 or j >= C_KERNEL:
                        continue
                    term = win[m_off:m_off + ROW_BLOCK + 8, :] * wdw_ref[j:j + 1, :]
                    part = term if part is None else part + term
                acc = acc + part[s_off:s_off + ROW_BLOCK, :]
            yc = _layer_norm(acc, vec(_V_GCONV), vec(_V_BCONV))
            mix_s[rows, 2 * GROUP_W:3 * GROUP_W] = (yc * _sigmoid(yc)).astype(BF16)

            o_d = (_dot(csl_ref[rows, 0:L], zc_s[...])
                   + _dot(csl_ref[rows, L:2 * L], zs_s[...]))
            mix_s[rows, 3 * GROUP_W:4 * GROUP_W] = o_d.astype(BF16)

            x = x_ref[rows, :]
            x1 = x + gt1 * _dot(mix_s[rows, :], w_out_ref[...])
            h2 = (_rms(x1, g_mlp) * (1.0 + sc2) + sh2).astype(BF16)
            f = jnp.maximum(_dot(h2, w1_ref[...]), 0.0)
            f = (f * f).astype(BF16)
            xo_ref[rows, :] = x1 + gt2 * _dot(f, w2_ref[...])

        for_each_block(phase1)
        for_each_block(phase2)

    for si in range(n_sub):
        run_sequence(si)


def _trunk_layers(x, layer0, n_layers, mod_all, params, consts, weights, latent_inputs,
                  kv_bufs=None):
    assert n_layers == 1
    n_seq, L, _ = x.shape
    latent = latent_inputs is not None
    past_len = latent_inputs[0].shape[2] if latent else 0
    n_sub = 1 if latent else CTX_SEQS_PER_STEP
    bdcs, csl = consts

    def layer_spec(arr, **kw):
        nd = arr.ndim - 1
        return pl.BlockSpec((None,) + arr.shape[1:],
                            lambda l, b, _n=nd: (layer0 + l,) + (0,) * _n, **kw)

    def const_spec(arr):
        return pl.BlockSpec(arr.shape, lambda l, b, _n=arr.ndim: (0,) * _n,
                            pipeline_mode=pl.Buffered(1))

    x_mode = dict(pipeline_mode=pl.Buffered(1)) if latent else {}
    w_mode = dict(pipeline_mode=pl.Buffered(1))
    x_spec = pl.BlockSpec((n_sub, L, D_MODEL), lambda l, b: (b, 0, 0), **x_mode)
    in_specs = ([x_spec, layer_spec(mod_all)]
                + [layer_spec(p) for p in params]
                + [const_spec(bdcs), const_spec(csl)]
                + [layer_spec(w, **w_mode) for w in weights])
    args = [x, mod_all, *params, bdcs, csl, *weights]
    out_specs = [x_spec]
    out_shape = [jax.ShapeDtypeStruct(x.shape, F32)]
    if latent:
        cache_k, cache_v, cos_t, sin_t = latent_inputs
        kv_spec = pl.BlockSpec((None, None, past_len, GROUP_W),
                               lambda l, b: (b, layer0 + l, 0, 0))
        in_specs += [kv_spec, kv_spec, const_spec(cos_t), const_spec(sin_t)]
        args += [cache_k, cache_v, cos_t, sin_t]
    else:
        kv_out = pl.BlockSpec((n_sub, None, L, GROUP_W), lambda l, b: (b, layer0 + l, 0, 0))
        out_specs += [kv_out, kv_out]
        out_shape += [jax.ShapeDtypeStruct((n_seq, DEPTH, L, GROUP_W), F32)] * 2

    lk = past_len + L
    scratch = [
        pltpu.VMEM((n_sub, L, GROUP_W), F32),
        pltpu.VMEM((n_sub, lk, GROUP_W), BF16),
        pltpu.VMEM((n_sub, lk, GROUP_W), BF16),
        pltpu.VMEM((n_sub, L, GROUP_W), F32),
        pltpu.VMEM((n_sub, L, GROUP_W), BF16),
        pltpu.VMEM((n_sub, L + 2 * CONV_PAD, GROUP_W), F32),
        pltpu.VMEM((n_sub, L, GROUP_W), BF16),
        pltpu.VMEM((n_sub, L, GROUP_W), BF16),
        pltpu.VMEM((n_sub, L, D_MODEL), BF16),
    ]
    aliases = {}
    if kv_bufs is not None:
        aliases = {len(args): 1, len(args) + 1: 2}
        in_specs += [pl.BlockSpec(memory_space=pl.ANY)] * 2
        args += list(kv_bufs)
    kern = functools.partial(_layer_kernel, seq_len=L, past_len=past_len,
                             latent=latent, n_sub=n_sub,
                             n_alias_in=0 if kv_bufs is None else 2)
    return pl.pallas_call(
        kern,
        grid=(n_layers, n_seq // n_sub),
        in_specs=in_specs,
        out_specs=out_specs,
        out_shape=out_shape,
        scratch_shapes=scratch,
        input_output_aliases=aliases,
        compiler_params=pltpu.CompilerParams(
            dimension_semantics=("arbitrary", "arbitrary"),
            vmem_limit_bytes=VMEM_LIMIT),
        name=("latent" if latent else "context") + f"_layers{layer0}_{layer0 + n_layers}",
    )(*args)


def _dft_tables(L):
    nc = GROUP_W // D_GROUPS
    cc = np.arange(nc)
    ang_c = 2.0 * np.pi * ((cc[:, None] * cc[None, :]) % nc) / nc
    bd_c = np.kron(np.eye(D_GROUPS), np.cos(ang_c) / math.sqrt(nc))
    bd_s = np.kron(np.eye(D_GROUPS), np.sin(ang_c) / math.sqrt(nc))
    bdcs = np.concatenate([bd_c, bd_s], axis=1)
    pp = np.arange(L)
    ang_l = 2.0 * np.pi * ((pp[:, None] * pp[None, :]) % L) / L
    csl = np.concatenate([np.cos(ang_l), -np.sin(ang_l)], axis=1) / math.sqrt(L)
    return jnp.asarray(bdcs, F32).astype(BF16), jnp.asarray(csl, F32).astype(BF16)


def _rope_tables(n_tokens):
    rows = n_tokens // GRID_W
    row = jnp.repeat(jnp.arange(rows, dtype=F32), GRID_W)
    col = jnp.tile(jnp.arange(GRID_W, dtype=F32), rows)
    nf = A_QK_DIM // 4
    inv = ROPE_BASE ** (-jnp.arange(nf, dtype=F32) / nf)
    ang = jnp.stack([row[:, None] * inv, col[:, None] * inv], axis=1)
    cos, sin = jnp.cos(ang), jnp.sin(ang)
    cos32 = jnp.stack([cos, cos], axis=2).reshape(n_tokens, A_QK_DIM)
    sin32 = jnp.stack([-sin, sin], axis=2).reshape(n_tokens, A_QK_DIM)
    reps = GROUP_W // A_QK_DIM
    return jnp.tile(cos32, (1, reps)), jnp.tile(sin32, (1, reps))


def kernel(x_prompt, x_sample, c, cache_k, cache_v, c_ctx, w_ada, b_ada, g_attn_norm, g_mlp_norm, w_in, g_q, g_k, lam_q1, lam_k1, lam_q2, lam_k2, g_head, g_sg, b_sg, w_spatial, b_spatial, w_dw, b_dw, g_conv, b_conv, w_out, w_ff1, w_ff2):
    n_p, l_p, _ = x_prompt.shape
    n_s, l_s, _ = x_sample.shape
    past_len = cache_k.shape[2]

    cond = jnp.concatenate(
        [c_ctx[None, :], c, jnp.zeros((ADA_ROWS - 1 - n_s, D_MODEL), F32)], axis=0)
    mod_all = _ada_modulation(cond, w_ada, b_ada)

    w_in_b = w_in.astype(BF16)
    w_out_b = w_out.astype(BF16)
    w1_b = w_ff1.astype(BF16)
    w2_b = w_ff2.astype(BF16)

    bdcs, csl_p = _dft_tables(l_p)
    _, csl_s = _dft_tables(l_s)
    cos_t, sin_t = _rope_tables(l_s)
    ck4 = cache_k.reshape(n_s, DEPTH, past_len, GROUP_W)
    cv4 = cache_v.reshape(n_s, DEPTH, past_len, GROUP_W)

    v1024 = jnp.stack([g_attn_norm, g_mlp_norm], axis=1)
    rows = [jnp.tile(g_q, (1, GROUP_W // A_QK_DIM)), jnp.tile(g_k, (1, GROUP_W // A_QK_DIM)),
            jnp.tile(g_head, (1, GROUP_W // A_V_DIM)), g_sg, b_sg, b_dw, g_conv, b_conv]
    v256 = jnp.concatenate(
        [jnp.stack(rows, axis=1), jnp.zeros((DEPTH, 16 - len(rows), GROUP_W), F32)], axis=1)
    lam_init = jnp.asarray(
        [0.8 - 0.6 * math.exp(-0.3 * l) for l in range(DEPTH)], F32)
    lam5 = jnp.stack([lam_q1, lam_k1, lam_q2, lam_k2,
                      jnp.broadcast_to(lam_init[:, None], lam_q1.shape)], axis=1)
    lam5 = jnp.concatenate([lam5, jnp.zeros((DEPTH, 3, A_QK_DIM), F32)], axis=1)
    wsp = w_spatial.reshape(DEPTH, B_GROUPS * CHUNK, CHUNK).astype(BF16)
    bsp = jnp.repeat(jnp.swapaxes(b_spatial, 1, 2), GROUP_W // B_GROUPS, axis=2)
    wdw = jnp.concatenate([w_dw, jnp.zeros((DEPTH, 1, GROUP_W), F32)], axis=1)
    params = (v1024, v256, lam5, wsp, bsp, wdw)
    weights = (w_in_b, w_out_b, w1_b, w2_b)

    xp, xs = x_prompt, x_sample
    kv_bufs = tuple(jnp.zeros((n_p, DEPTH, l_p, GROUP_W), F32) for _ in range(2))
    for l in range(DEPTH):
        xp, new_k, new_v = _trunk_layers(
            xp, l, 1, mod_all, params, (bdcs, csl_p), weights, None, kv_bufs)
        kv_bufs = (new_k, new_v)
        xs = _trunk_layers(xs, l, 1, mod_all, params, (bdcs, csl_s), weights,
                           (ck4, cv4, cos_t, sin_t))[0]

    new_k = new_k.reshape(n_p, DEPTH, l_p, A_HEADS, 2 * A_QK_DIM)
    new_v = new_v.reshape(n_p, DEPTH, l_p, A_HEADS, A_V_DIM)
    return (xp, xs, new_k, new_v)
```

```python
import functools
import math

import numpy as np
import jax
import jax.numpy as jnp
from jax import lax
from jax.experimental import pallas as pl
from jax.experimental.pallas import tpu as pltpu

F32 = jnp.float32
BF16 = jnp.bfloat16

D_MODEL = 1024
DEPTH = 4
GRID_W = 64
GROUP_W = 256
A_HEADS = 4
A_V_DIM = 64
A_QK_DIM = 32
B_GROUPS = 4
CHUNK = 128
C_KERNEL = 31
D_GROUPS = 4
D_FF = 4 * D_MODEL
ROPE_BASE = 10000.0
EPS = 1e-6
IN_W = 8 * GROUP_W
LOG2_E = 1.4426950408889634

ROW_BLOCK = 256
CONV_PAD = 16
ADA_ROWS = 8
ADA_COLS = 2048
VMEM_LIMIT = 58 * 1024 * 1024

_V_GQ, _V_GK, _V_GHEAD, _V_GSG, _V_BSG, _V_BDW, _V_GCONV, _V_BCONV = range(8)

Q_SCALE = A_QK_DIM ** -0.5 * LOG2_E


def _dot(a, b):
    return jnp.dot(a, b, preferred_element_type=F32)


def _sigmoid(x):
    return 1.0 / (1.0 + jnp.exp(-x))


def _rms(x, g):
    ms = jnp.mean(x * x, axis=-1, keepdims=True)
    return x * lax.rsqrt(ms + EPS) * g


def _layer_norm(x, g, b):
    mu = jnp.mean(x, axis=-1, keepdims=True)
    xc = x - mu
    var = jnp.mean(xc * xc, axis=-1, keepdims=True)
    return xc * lax.rsqrt(var + EPS) * g + b


def _group_rms(x, gmat, gsize, g):
    ms = _dot((x * x).astype(BF16), gmat) * (1.0 / gsize)
    return x * lax.rsqrt(ms + EPS) * g


def _group_ones(gsize):
    row_i = lax.broadcasted_iota(jnp.int32, (GROUP_W, GROUP_W), 0)
    col_i = lax.broadcasted_iota(jnp.int32, (GROUP_W, GROUP_W), 1)
    return jnp.where(row_i // gsize == col_i // gsize, 1.0, 0.0).astype(BF16)


def _lambda(lam_ref):
    lam_v = lam_ref[...]
    l1 = jnp.sum(lam_v[0:1] * lam_v[1:2], axis=-1, keepdims=True)
    l2 = jnp.sum(lam_v[2:3] * lam_v[3:4], axis=-1, keepdims=True)
    lam_init = lam_v[4:5, 0:1]
    return jnp.exp(l1) - jnp.exp(l2) + lam_init, lam_init


def _attention_head(qf, kb, vb, hh, lam, lane):
    outs = []
    for c in range(2):
        lo = hh * A_V_DIM + c * A_QK_DIM
        m = (lane >= lo) & (lane < lo + A_QK_DIM)
        qm = jnp.where(m, qf, 0.0).astype(BF16)
        s = lax.dot_general(qm, kb, (((1,), (1,)), ((), ())),
                            preferred_element_type=F32)
        e = jnp.exp2(s - jnp.max(s, axis=-1, keepdims=True))
        inv = 1.0 / jnp.sum(e, axis=-1, keepdims=True)
        outs.append(_dot(e.astype(BF16), vb) * inv)
    return outs[0] - lam * outs[1]


def _spatial_gate(res, bias, lane):
    mixed = bias
    for g in range(B_GROUPS):
        gm = (lane >= g * (GROUP_W // B_GROUPS)) & (lane < (g + 1) * (GROUP_W // B_GROUPS))
        mixed = mixed + jnp.where(gm, res[g * CHUNK:(g + 1) * CHUNK, :], 0.0)
    return mixed


def _depthwise_conv(win, wdw_ref, bias, n_rows):
    acc = jnp.zeros((n_rows, GROUP_W), F32) + bias
    for s_off in range(8):
        part = None
        for m_off in range(0, 2 * CONV_PAD, 8):
            j = m_off + s_off - 1
            if j < 0 or j >= C_KERNEL:
                continue
            term = win[m_off:m_off + n_rows + 8, :] * wdw_ref[j:j + 1, :]
            part = term if part is None else part + term
        acc = acc + part[s_off:s_off + n_rows, :]
    return acc


def _ada_kernel(cond_ref, w_ref, b_ref, o_ref):
    cnd = cond_ref[...]
    s = (cnd * _sigmoid(cnd)).astype(BF16)
    o_ref[0] = _dot(s, w_ref[0].astype(BF16)) + b_ref[0]


def _ada_modulation(cond, w_ada, b_ada):
    ncol = w_ada.shape[-1]
    return pl.pallas_call(
        _ada_kernel,
        grid=(DEPTH, ncol // ADA_COLS),
        in_specs=[
            pl.BlockSpec((ADA_ROWS, D_MODEL), lambda l, j: (0, 0)),
            pl.BlockSpec((1, D_MODEL, ADA_COLS), lambda l, j: (l, 0, j)),
            pl.BlockSpec((1, 1, ADA_COLS), lambda l, j: (l, 0, j)),
        ],
        out_specs=pl.BlockSpec((1, ADA_ROWS, ADA_COLS), lambda l, j: (l, 0, j)),
        out_shape=jax.ShapeDtypeStruct((DEPTH, ADA_ROWS, ncol), F32),
        compiler_params=pltpu.CompilerParams(
            dimension_semantics=("arbitrary", "arbitrary"),
            vmem_limit_bytes=VMEM_LIMIT),
        name="ada_modulation",
    )(cond, w_ada, b_ada.reshape(DEPTH, 1, ncol))


def _ctx_kernel(xp_hbm, mod_ref, v1024_ref, v256_ref, lam_ref, wsp_ref, bsp_ref, wdw_ref,
                bdcs_ref, csl_ref, w_in_ref, w_out_ref, w1_ref, w2_ref,
                xo_hbm, nk_ref, nv_ref,
                xbuf, obuf, q_s, k_s, v_s, ub_s, vn_s, hp_s, zc_s, zs_s, mix_s,
                in_sem, out_sem, *, n_seq, seq_len):
    L = seq_len
    n_items = DEPTH * n_seq
    s = pl.program_id(0)
    slot = s % 2
    other = 1 - slot

    def fetch_copy(src_hbm, seq, dst_slot):
        return pltpu.make_async_copy(src_hbm.at[seq], xbuf.at[dst_slot], in_sem.at[dst_slot])

    def start_fetch(item, dst_slot):
        seq = item % n_seq

        @pl.when(item < n_seq)
        def _():
            fetch_copy(xp_hbm, seq, dst_slot).start()

        @pl.when(item >= n_seq)
        def _():
            fetch_copy(xo_hbm, seq, dst_slot).start()

    def out_copy(seq, src_slot):
        return pltpu.make_async_copy(obuf.at[src_slot], xo_hbm.at[seq], out_sem.at[src_slot])

    @pl.when(s == 0)
    def _():
        start_fetch(s, slot)

    fetch_copy(xp_hbm, 0, slot).wait()

    @pl.when(s + 1 < n_items)
    def _():
        start_fetch(s + 1, other)

    @pl.when(s >= 2)
    def _():
        out_copy(0, slot).wait()

    mod = mod_ref[0:1, :]
    sh1, sc1, gt1, sh2, sc2, gt2 = [
        mod[:, i * D_MODEL:(i + 1) * D_MODEL] for i in range(6)]
    g_attn = v1024_ref[0:1, :]
    g_mlp = v1024_ref[1:2, :]

    def vec(i):
        return v256_ref[i:i + 1, :]

    lane = lax.broadcasted_iota(jnp.int32, (1, GROUP_W), 1)
    gmat32 = _group_ones(A_QK_DIM)
    gmat64 = _group_ones(A_V_DIM)
    lam, lam_init = _lambda(lam_ref)

    x = xbuf[slot]
    h = _rms(x, g_attn) * (1.0 + sc1) + sh1
    z = _dot(h.astype(BF16), w_in_ref[...])
    zq = z[:, 0:GROUP_W]
    zk = z[:, GROUP_W:2 * GROUP_W]
    zv = z[:, 2 * GROUP_W:3 * GROUP_W]
    zb = z[:, 3 * GROUP_W:5 * GROUP_W]
    zc = z[:, 5 * GROUP_W:7 * GROUP_W]
    zd = z[:, 7 * GROUP_W:8 * GROUP_W]

    q = _group_rms(zq, gmat32, A_QK_DIM, vec(_V_GQ))
    k = _group_rms(zk, gmat32, A_QK_DIM, vec(_V_GK))
    nk_ref[...] = k
    nv_ref[...] = zv
    q_s[...] = q * Q_SCALE
    k_s[...] = k.astype(BF16)
    v_s[...] = zv.astype(BF16)

    gb = jax.nn.gelu(zb, approximate=True)
    ub_s[...] = gb[:, 0:GROUP_W]
    vn_s[...] = _layer_norm(gb[:, GROUP_W:], vec(_V_GSG), vec(_V_BSG)).astype(BF16)

    zero_pad = jnp.zeros((CONV_PAD, GROUP_W), F32)
    hp_s[0:CONV_PAD, :] = zero_pad
    hp_s[CONV_PAD + L:2 * CONV_PAD + L, :] = zero_pad
    hp_s[CONV_PAD:CONV_PAD + L, :] = zc[:, 0:GROUP_W] * _sigmoid(zc[:, GROUP_W:])

    zcs = _dot(zd.astype(BF16), bdcs_ref[...])
    zc_s[...] = zcs[:, 0:GROUP_W].astype(BF16)
    zs_s[...] = zcs[:, GROUP_W:].astype(BF16)

    qf = q_s[...]
    o = jnp.zeros((L, GROUP_W), F32)
    for hh in range(A_HEADS):
        oh = _attention_head(qf, k_s[...], v_s[...], hh, lam, lane)
        hm = (lane >= hh * A_V_DIM) & (lane < (hh + 1) * A_V_DIM)
        o = jnp.where(hm, oh, o)
    o_a = _group_rms(o, gmat64, A_V_DIM, vec(_V_GHEAD)) * (1.0 - lam_init)
    mix_s[:, 0:GROUP_W] = o_a.astype(BF16)

    for c in range(L // CHUNK):
        crow = slice(c * CHUNK, (c + 1) * CHUNK)
        mixed = _spatial_gate(_dot(wsp_ref[...], vn_s[crow, :]), bsp_ref[...], lane)
        mix_s[crow, GROUP_W:2 * GROUP_W] = (ub_s[crow, :] * mixed).astype(BF16)

    yc = _layer_norm(_depthwise_conv(hp_s[...], wdw_ref, vec(_V_BDW), L),
                     vec(_V_GCONV), vec(_V_BCONV))
    mix_s[:, 2 * GROUP_W:3 * GROUP_W] = (yc * _sigmoid(yc)).astype(BF16)

    o_d = _dot(csl_ref[:, 0:L], zc_s[...]) + _dot(csl_ref[:, L:2 * L], zs_s[...])
    mix_s[:, 3 * GROUP_W:4 * GROUP_W] = o_d.astype(BF16)

    x1 = x + gt1 * _dot(mix_s[...], w_out_ref[...])
    h2 = (_rms(x1, g_mlp) * (1.0 + sc2) + sh2).astype(BF16)
    f = jnp.maximum(_dot(h2, w1_ref[...]), 0.0)
    f = (f * f).astype(BF16)
    obuf[slot] = x1 + gt2 * _dot(f, w2_ref[...])

    out_copy(s % n_seq, slot).start()

    @pl.when(s == n_items - 1)
    def _():
        out_copy(0, slot).wait()
        out_copy(0, other).wait()


def _context_trunk(x, mod_all, params, consts, weights):
    n_seq, L, _ = x.shape
    n_items = DEPTH * n_seq
    assert n_seq >= 4
    bdcs, csl = consts

    def layer_spec(arr):
        nd = arr.ndim - 1
        return pl.BlockSpec((None,) + arr.shape[1:],
                            lambda s, _n=nd: (s // n_seq,) + (0,) * _n)

    def const_spec(arr):
        return pl.BlockSpec(arr.shape, lambda s, _n=arr.ndim: (0,) * _n,
                            pipeline_mode=pl.Buffered(1))

    in_specs = ([pl.BlockSpec(memory_space=pl.ANY), layer_spec(mod_all)]
                + [layer_spec(p) for p in params]
                + [const_spec(bdcs), const_spec(csl)]
                + [layer_spec(w) for w in weights])
    args = [x, mod_all, *params, bdcs, csl, *weights]
    kv_out = pl.BlockSpec((None, None, L, GROUP_W), lambda s: (s % n_seq, s // n_seq, 0, 0))
    out_specs = [pl.BlockSpec(memory_space=pl.ANY), kv_out, kv_out]
    out_shape = [jax.ShapeDtypeStruct(x.shape, F32)] + [
        jax.ShapeDtypeStruct((n_seq, DEPTH, L, GROUP_W), F32)] * 2
    scratch = [
        pltpu.VMEM((2, L, D_MODEL), F32),
        pltpu.VMEM((2, L, D_MODEL), F32),
        pltpu.VMEM((L, GROUP_W), F32),
        pltpu.VMEM((L, GROUP_W), BF16),
        pltpu.VMEM((L, GROUP_W), BF16),
        pltpu.VMEM((L, GROUP_W), F32),
        pltpu.VMEM((L, GROUP_W), BF16),
        pltpu.VMEM((L + 2 * CONV_PAD, GROUP_W), F32),
        pltpu.VMEM((L, GROUP_W), BF16),
        pltpu.VMEM((L, GROUP_W), BF16),
        pltpu.VMEM((L, D_MODEL), BF16),
        pltpu.SemaphoreType.DMA((2,)),
        pltpu.SemaphoreType.DMA((2,)),
    ]
    kern = functools.partial(_ctx_kernel, n_seq=n_seq, seq_len=L)
    return pl.pallas_call(
        kern,
        grid=(n_items,),
        in_specs=in_specs,
        out_specs=out_specs,
        out_shape=out_shape,
        scratch_shapes=scratch,
        compiler_params=pltpu.CompilerParams(
            dimension_semantics=("arbitrary",),
            vmem_limit_bytes=VMEM_LIMIT),
        name="context_trunk",
    )(*args)


def _latent_kernel(x_ref, mod_ref, v1024_ref, v256_ref, lam_ref, wsp_ref, bsp_ref, wdw_ref,
                   bdcs_ref, csl_ref, w_in_ref, w_out_ref, w1_ref, w2_ref,
                   ck_ref, cv_ref, cos_ref, sin_ref,
                   xo_ref,
                   q_s, k_s, v_s, ub_s, vn_s, hp_s, zc_s, zs_s, mix_s, *, seq_len, past_len):
    L = seq_len
    n_blk = L // ROW_BLOCK

    mod = mod_ref[pl.ds(1 + pl.program_id(0), 1), :]
    sh1, sc1, gt1, sh2, sc2, gt2 = [
        mod[:, i * D_MODEL:(i + 1) * D_MODEL] for i in range(6)]
    g_attn = v1024_ref[0:1, :]
    g_mlp = v1024_ref[1:2, :]

    def vec(i):
        return v256_ref[i:i + 1, :]

    lane = lax.broadcasted_iota(jnp.int32, (1, GROUP_W), 1)
    gmat32 = _group_ones(A_QK_DIM)
    gmat64 = _group_ones(A_V_DIM)
    lam, lam_init = _lambda(lam_ref)
    swap_lo = (lane // (A_QK_DIM // 4)) % 2 == 0

    zero_pad = jnp.zeros((CONV_PAD, GROUP_W), F32)
    hp_s[0:CONV_PAD, :] = zero_pad
    hp_s[CONV_PAD + L:2 * CONV_PAD + L, :] = zero_pad
    k_s[0:past_len, :] = ck_ref[...].astype(BF16)
    v_s[0:past_len, :] = cv_ref[...].astype(BF16)

    def phase1(r0):
        rows = pl.ds(r0, ROW_BLOCK)
        x = x_ref[0, rows, :]
        h = _rms(x, g_attn) * (1.0 + sc1) + sh1
        z = _dot(h.astype(BF16), w_in_ref[...])
        zq = z[:, 0:GROUP_W]
        zk = z[:, GROUP_W:2 * GROUP_W]
        zv = z[:, 2 * GROUP_W:3 * GROUP_W]
        zb = z[:, 3 * GROUP_W:5 * GROUP_W]
        zc = z[:, 5 * GROUP_W:7 * GROUP_W]
        zd = z[:, 7 * GROUP_W:8 * GROUP_W]

        cs = cos_ref[rows, :]
        sn = sin_ref[rows, :]

        def rope(t):
            swapped = jnp.where(
                swap_lo,
                pltpu.roll(t, GROUP_W - A_QK_DIM // 4, axis=1),
                pltpu.roll(t, A_QK_DIM // 4, axis=1))
            return t * cs + swapped * sn

        q = rope(_group_rms(zq, gmat32, A_QK_DIM, vec(_V_GQ)))
        k = rope(_group_rms(zk, gmat32, A_QK_DIM, vec(_V_GK)))
        q_s[rows, :] = q * Q_SCALE
        k_s[pl.ds(past_len + r0, ROW_BLOCK), :] = k.astype(BF16)
        v_s[pl.ds(past_len + r0, ROW_BLOCK), :] = zv.astype(BF16)

        gb = jax.nn.gelu(zb, approximate=True)
        ub_s[rows, :] = gb[:, 0:GROUP_W]
        vn_s[rows, :] = _layer_norm(
            gb[:, GROUP_W:], vec(_V_GSG), vec(_V_BSG)).astype(BF16)

        hp_s[pl.ds(CONV_PAD + r0, ROW_BLOCK), :] = (
            zc[:, 0:GROUP_W] * _sigmoid(zc[:, GROUP_W:]))

        zcs = _dot(zd.astype(BF16), bdcs_ref[...])
        zc_s[rows, :] = zcs[:, 0:GROUP_W].astype(BF16)
        zs_s[rows, :] = zcs[:, GROUP_W:].astype(BF16)

    def phase2(r0):
        rows = pl.ds(r0, ROW_BLOCK)

        qf = q_s[rows, :]

        def head(hh, o):
            oh = _attention_head(qf, k_s[...], v_s[...], hh, lam, lane)
            hm = (lane >= hh * A_V_DIM) & (lane < (hh + 1) * A_V_DIM)
            return jnp.where(hm, oh, o)

        o = lax.fori_loop(0, A_HEADS, head, jnp.zeros((ROW_BLOCK, GROUP_W), F32),
                          unroll=True)
        o_a = _group_rms(o, gmat64, A_V_DIM, vec(_V_GHEAD)) * (1.0 - lam_init)
        mix_s[rows, 0:GROUP_W] = o_a.astype(BF16)

        for c in range(ROW_BLOCK // CHUNK):
            crow = pl.ds(r0 + c * CHUNK, CHUNK)
            mixed = _spatial_gate(_dot(wsp_ref[...], vn_s[crow, :]), bsp_ref[...], lane)
            mix_s[crow, GROUP_W:2 * GROUP_W] = (ub_s[crow, :] * mixed).astype(BF16)

        win = hp_s[pl.ds(r0, ROW_BLOCK + 2 * CONV_PAD), :]
        yc = _layer_norm(_depthwise_conv(win, wdw_ref, vec(_V_BDW), ROW_BLOCK),
                         vec(_V_GCONV), vec(_V_BCONV))
        mix_s[rows, 2 * GROUP_W:3 * GROUP_W] = (yc * _sigmoid(yc)).astype(BF16)

        o_d = (_dot(csl_ref[rows, 0:L], zc_s[...])
               + _dot(csl_ref[rows, L:2 * L], zs_s[...]))
        mix_s[rows, 3 * GROUP_W:4 * GROUP_W] = o_d.astype(BF16)

        x = x_ref[0, rows, :]
        x1 = x + gt1 * _dot(mix_s[rows, :], w_out_ref[...])
        h2 = (_rms(x1, g_mlp) * (1.0 + sc2) + sh2).astype(BF16)
        f = jnp.maximum(_dot(h2, w1_ref[...]), 0.0)
        f = (f * f).astype(BF16)
        xo_ref[0, rows, :] = x1 + gt2 * _dot(f, w2_ref[...])

    def for_each_block(phase):
        def step(r, carry):
            phase(pl.multiple_of(r * ROW_BLOCK, ROW_BLOCK))
            return carry
        lax.fori_loop(0, n_blk, step, 0)

    for_each_block(phase1)
    for_each_block(phase2)


def _latent_layer(x, layer, mod_all, params, consts, weights, cache_k, cache_v, cos_t, sin_t):
    n_seq, L, _ = x.shape
    past_len = cache_k.shape[2]
    bdcs, csl = consts
    single = dict(pipeline_mode=pl.Buffered(1))

    def layer_spec(arr):
        nd = arr.ndim - 1
        return pl.BlockSpec((None,) + arr.shape[1:],
                            lambda b, _n=nd: (layer,) + (0,) * _n, **single)

    def const_spec(arr):
        return pl.BlockSpec(arr.shape, lambda b, _n=arr.ndim: (0,) * _n, **single)

    x_spec = pl.BlockSpec((1, L, D_MODEL), lambda b: (b, 0, 0), **single)
    kv_spec = pl.BlockSpec((None, None, past_len, GROUP_W), lambda b: (b, layer, 0, 0))
    in_specs = ([x_spec, layer_spec(mod_all)]
                + [layer_spec(p) for p in params]
                + [const_spec(bdcs), const_spec(csl)]
                + [layer_spec(w) for w in weights]
                + [kv_spec, kv_spec, const_spec(cos_t), const_spec(sin_t)])
    args = [x, mod_all, *params, bdcs, csl, *weights, cache_k, cache_v, cos_t, sin_t]
    lk = past_len + L
    scratch = [
        pltpu.VMEM((L, GROUP_W), F32),
        pltpu.VMEM((lk, GROUP_W), BF16),
        pltpu.VMEM((lk, GROUP_W), BF16),
        pltpu.VMEM((L, GROUP_W), F32),
        pltpu.VMEM((L, GROUP_W), BF16),
        pltpu.VMEM((L + 2 * CONV_PAD, GROUP_W), F32),
        pltpu.VMEM((L, GROUP_W), BF16),
        pltpu.VMEM((L, GROUP_W), BF16),
        pltpu.VMEM((L, D_MODEL), BF16),
    ]
    kern = functools.partial(_latent_kernel, seq_len=L, past_len=past_len)
    return pl.pallas_call(
        kern,
        grid=(n_seq,),
        in_specs=in_specs,
        out_specs=x_spec,
        out_shape=jax.ShapeDtypeStruct(x.shape, F32),
        scratch_shapes=scratch,
        compiler_params=pltpu.CompilerParams(
            dimension_semantics=("arbitrary",),
            vmem_limit_bytes=VMEM_LIMIT),
        name=f"latent_layer{layer}",
    )(*args)


def _dft_tables(L):
    nc = GROUP_W // D_GROUPS
    cc = np.arange(nc)
    ang_c = 2.0 * np.pi * ((cc[:, None] * cc[None, :]) % nc) / nc
    bd_c = np.kron(np.eye(D_GROUPS), np.cos(ang_c) / math.sqrt(nc))
    bd_s = np.kron(np.eye(D_GROUPS), np.sin(ang_c) / math.sqrt(nc))
    bdcs = np.concatenate([bd_c, bd_s], axis=1)
    pp = np.arange(L)
    ang_l = 2.0 * np.pi * ((pp[:, None] * pp[None, :]) % L) / L
    csl = np.concatenate([np.cos(ang_l), -np.sin(ang_l)], axis=1) / math.sqrt(L)
    return jnp.asarray(bdcs, F32).astype(BF16), jnp.asarray(csl, F32).astype(BF16)


def _rope_tables(n_tokens):
    rows = n_tokens // GRID_W
    row = jnp.repeat(jnp.arange(rows, dtype=F32), GRID_W)
    col = jnp.tile(jnp.arange(GRID_W, dtype=F32), rows)
    nf = A_QK_DIM // 4
    inv = ROPE_BASE ** (-jnp.arange(nf, dtype=F32) / nf)
    ang = jnp.stack([row[:, None] * inv, col[:, None] * inv], axis=1)
    cos, sin = jnp.cos(ang), jnp.sin(ang)
    cos32 = jnp.stack([cos, cos], axis=2).reshape(n_tokens, A_QK_DIM)
    sin32 = jnp.stack([-sin, sin], axis=2).reshape(n_tokens, A_QK_DIM)
    reps = GROUP_W // A_QK_DIM
    return jnp.tile(cos32, (1, reps)), jnp.tile(sin32, (1, reps))


def kernel(x_prompt, x_sample, c, cache_k, cache_v, c_ctx, w_ada, b_ada, g_attn_norm, g_mlp_norm, w_in, g_q, g_k, lam_q1, lam_k1, lam_q2, lam_k2, g_head, g_sg, b_sg, w_spatial, b_spatial, w_dw, b_dw, g_conv, b_conv, w_out, w_ff1, w_ff2):
    n_p, l_p, _ = x_prompt.shape
    n_s, l_s, _ = x_sample.shape
    past_len = cache_k.shape[2]

    cond = jnp.concatenate(
        [c_ctx[None, :], c, jnp.zeros((ADA_ROWS - 1 - n_s, D_MODEL), F32)], axis=0)
    mod_all = _ada_modulation(cond, w_ada, b_ada)

    w_in_b = w_in.astype(BF16)
    w_out_b = w_out.astype(BF16)
    w1_b = w_ff1.astype(BF16)
    w2_b = w_ff2.astype(BF16)

    bdcs, csl_p = _dft_tables(l_p)
    _, csl_s = _dft_tables(l_s)
    cos_t, sin_t = _rope_tables(l_s)
    ck4 = cache_k.reshape(n_s, DEPTH, past_len, GROUP_W)
    cv4 = cache_v.reshape(n_s, DEPTH, past_len, GROUP_W)

    v1024 = jnp.stack([g_attn_norm, g_mlp_norm], axis=1)
    rows = [jnp.tile(g_q, (1, GROUP_W // A_QK_DIM)), jnp.tile(g_k, (1, GROUP_W // A_QK_DIM)),
            jnp.tile(g_head, (1, GROUP_W // A_V_DIM)), g_sg, b_sg, b_dw, g_conv, b_conv]
    v256 = jnp.concatenate(
        [jnp.stack(rows, axis=1), jnp.zeros((DEPTH, 16 - len(rows), GROUP_W), F32)], axis=1)
    lam_init = jnp.asarray(
        [0.8 - 0.6 * math.exp(-0.3 * l) for l in range(DEPTH)], F32)
    lam5 = jnp.stack([lam_q1, lam_k1, lam_q2, lam_k2,
                      jnp.broadcast_to(lam_init[:, None], lam_q1.shape)], axis=1)
    lam5 = jnp.concatenate([lam5, jnp.zeros((DEPTH, 3, A_QK_DIM), F32)], axis=1)
    wsp = w_spatial.reshape(DEPTH, B_GROUPS * CHUNK, CHUNK).astype(BF16)
    bsp = jnp.repeat(jnp.swapaxes(b_spatial, 1, 2), GROUP_W // B_GROUPS, axis=2)
    wdw = jnp.concatenate([w_dw, jnp.zeros((DEPTH, 1, GROUP_W), F32)], axis=1)
    params = (v1024, v256, lam5, wsp, bsp, wdw)
    weights = (w_in_b, w_out_b, w1_b, w2_b)

    xp, new_k, new_v = _context_trunk(x_prompt, mod_all, params, (bdcs, csl_p), weights)
    xs = x_sample
    for l in range(DEPTH):
        xs = _latent_layer(xs, l, mod_all, params, (bdcs, csl_s), weights,
                           ck4, cv4, cos_t, sin_t)

    new_k = new_k.reshape(n_p, DEPTH, l_p, A_HEADS, 2 * A_QK_DIM)
    new_v = new_v.reshape(n_p, DEPTH, l_p, A_HEADS, A_V_DIM)
    return (xp, xs, new_k, new_v)
```

```python
import functools
import math

import numpy as np
import jax
import jax.numpy as jnp
from jax import lax
from jax.experimental import pallas as pl
from jax.experimental.pallas import tpu as pltpu

F32 = jnp.float32
BF16 = jnp.bfloat16

D_MODEL = 1024
DEPTH = 4
GRID_W = 64
GROUP_W = 256
A_HEADS = 4
A_V_DIM = 64
A_QK_DIM = 32
B_GROUPS = 4
CHUNK = 128
C_KERNEL = 31
D_GROUPS = 4
D_FF = 4 * D_MODEL
ROPE_BASE = 10000.0
EPS = 1e-6
IN_W = 8 * GROUP_W
LOG2_E = 1.4426950408889634

ROW_BLOCK = 256
CONV_PAD = 16
ADA_ROWS = 8
ADA_COLS = 2048
VMEM_LIMIT = 58 * 1024 * 1024
CTX_VMEM_LIMIT = 62 * 1024 * 1024

_V_GQ, _V_GK, _V_GHEAD, _V_GSG, _V_BSG, _V_BDW, _V_GCONV, _V_BCONV = range(8)

Q_SCALE = A_QK_DIM ** -0.5 * LOG2_E


def _dot(a, b):
    return jnp.dot(a, b, preferred_element_type=F32)


def _sigmoid(x):
    return 1.0 / (1.0 + jnp.exp(-x))


def _rms(x, g):
    ms = jnp.mean(x * x, axis=-1, keepdims=True)
    return x * lax.rsqrt(ms + EPS) * g


def _layer_norm(x, g, b):
    mu = jnp.mean(x, axis=-1, keepdims=True)
    xc = x - mu
    var = jnp.mean(xc * xc, axis=-1, keepdims=True)
    return xc * lax.rsqrt(var + EPS) * g + b


def _group_rms(x, gmat, gsize, g):
    ms = _dot((x * x).astype(BF16), gmat) * (1.0 / gsize)
    return x * lax.rsqrt(ms + EPS) * g


def _group_ones(gsize):
    row_i = lax.broadcasted_iota(jnp.int32, (GROUP_W, GROUP_W), 0)
    col_i = lax.broadcasted_iota(jnp.int32, (GROUP_W, GROUP_W), 1)
    return jnp.where(row_i // gsize == col_i // gsize, 1.0, 0.0).astype(BF16)


def _lambda(lam_ref):
    lam_v = lam_ref[...]
    l1 = jnp.sum(lam_v[0:1] * lam_v[1:2], axis=-1, keepdims=True)
    l2 = jnp.sum(lam_v[2:3] * lam_v[3:4], axis=-1, keepdims=True)
    lam_init = lam_v[4:5, 0:1]
    return jnp.exp(l1) - jnp.exp(l2) + lam_init, lam_init


def _attention_head(qf, kb, vb, hh, lam, lane):
    outs = []
    for c in range(2):
        lo = hh * A_V_DIM + c * A_QK_DIM
        m = (lane >= lo) & (lane < lo + A_QK_DIM)
        qm = jnp.where(m, qf, 0.0).astype(BF16)
        s = lax.dot_general(qm, kb, (((1,), (1,)), ((), ())),
                            preferred_element_type=F32)
        e = jnp.exp2(s - jnp.max(s, axis=-1, keepdims=True))
        inv = 1.0 / jnp.sum(e, axis=-1, keepdims=True)
        outs.append(_dot(e.astype(BF16), vb) * inv)
    return outs[0] - lam * outs[1]


def _spatial_gate(res, bias, lane):
    mixed = bias
    for g in range(B_GROUPS):
        gm = (lane >= g * (GROUP_W // B_GROUPS)) & (lane < (g + 1) * (GROUP_W // B_GROUPS))
        mixed = mixed + jnp.where(gm, res[g * CHUNK:(g + 1) * CHUNK, :], 0.0)
    return mixed


def _depthwise_conv(win, wdw_ref, bias, n_rows):
    acc = jnp.zeros((n_rows, GROUP_W), F32) + bias
    for s_off in range(8):
        part = None
        for m_off in range(0, 2 * CONV_PAD, 8):
            j = m_off + s_off - 1
            if j < 0 or j >= C_KERNEL:
                continue
            term = win[m_off:m_off + n_rows + 8, :] * wdw_ref[j:j + 1, :]
            part = term if part is None else part + term
        acc = acc + part[s_off:s_off + n_rows, :]
    return acc


def _ada_kernel(cond_ref, w_ref, b_ref, o_ref):
    cnd = cond_ref[...]
    s = (cnd * _sigmoid(cnd)).astype(BF16)
    o_ref[0] = _dot(s, w_ref[0].astype(BF16)) + b_ref[0]


def _ada_modulation(cond, w_ada, b_ada):
    ncol = w_ada.shape[-1]
    return pl.pallas_call(
        _ada_kernel,
        grid=(DEPTH, ncol // ADA_COLS),
        in_specs=[
            pl.BlockSpec((ADA_ROWS, D_MODEL), lambda l, j: (0, 0)),
            pl.BlockSpec((1, D_MODEL, ADA_COLS), lambda l, j: (l, 0, j)),
            pl.BlockSpec((1, 1, ADA_COLS), lambda l, j: (l, 0, j)),
        ],
        out_specs=pl.BlockSpec((1, ADA_ROWS, ADA_COLS), lambda l, j: (l, 0, j)),
        out_shape=jax.ShapeDtypeStruct((DEPTH, ADA_ROWS, ncol), F32),
        compiler_params=pltpu.CompilerParams(
            dimension_semantics=("arbitrary", "arbitrary"),
            vmem_limit_bytes=VMEM_LIMIT),
        name="ada_modulation",
    )(cond, w_ada, b_ada.reshape(DEPTH, 1, ncol))


def _ctx_kernel(xp_hbm, mod_ref, v1024_ref, v256_ref, lam_ref, wsp_ref, bsp_ref, wdw_ref,
                bdcs_ref, csl_ref,
                w_in0_hbm, w_out0_hbm, w1_0_hbm, w2_0_hbm,
                w_in_f32, w_out_f32, w1_f32, w2_f32,
                xo_hbm, nk_ref, nv_ref, w_in_o, w_out_o, w1_o, w2_o,
                xbuf, obuf, q_s, k_s, v_s, ub_s, vn_s, hp_s, zc_s, zs_s, mix_s,
                w_in_s, w_out_s, w1_s, w2_s,
                in_sem, out_sem, w_sem, *, n_seq, seq_len):
    L = seq_len
    n_items = DEPTH * n_seq
    s = pl.program_id(0)
    slot = s % 2
    other = 1 - slot

    w_cur = (s // n_seq) % 2
    w_nxt = 1 - w_cur
    chunk = s % n_seq
    layer0 = ((w_in0_hbm, w_in_s), (w_out0_hbm, w_out_s), (w1_0_hbm, w1_s), (w2_0_hbm, w2_s))

    @pl.when(s == 0)
    def _():
        copies = [pltpu.make_async_copy(src, dst.at[0], w_sem.at[i])
                  for i, (src, dst) in enumerate(layer0)]
        for cp in copies:
            cp.start()
        for cp in copies:
            cp.wait()

    for f32_ref, out_ref, dst in ((w_in_f32, w_in_o, w_in_s), (w_out_f32, w_out_o, w_out_s),
                                  (w1_f32, w1_o, w1_s), (w2_f32, w2_o, w2_s)):
        rows = f32_ref.shape[0]
        piece = f32_ref[...].astype(BF16)
        out_ref[...] = piece
        dst[w_nxt, pl.ds(pl.multiple_of(chunk * rows, rows), rows), :] = piece

    w_in_ref = w_in_s.at[w_cur]
    w_out_ref = w_out_s.at[w_cur]
    w1_ref = w1_s.at[w_cur]
    w2_ref = w2_s.at[w_cur]

    def fetch_copy(src_hbm, seq, dst_slot):
        return pltpu.make_async_copy(src_hbm.at[seq], xbuf.at[dst_slot], in_sem.at[dst_slot])

    def start_fetch(item, dst_slot):
        seq = item % n_seq

        @pl.when(item < n_seq)
        def _():
            fetch_copy(xp_hbm, seq, dst_slot).start()

        @pl.when(item >= n_seq)
        def _():
            fetch_copy(xo_hbm, seq, dst_slot).start()

    def out_copy(seq, src_slot):
        return pltpu.make_async_copy(obuf.at[src_slot], xo_hbm.at[seq], out_sem.at[src_slot])

    @pl.when(s == 0)
    def _():
        start_fetch(s, slot)

    fetch_copy(xp_hbm, 0, slot).wait()

    @pl.when(s + 1 < n_items)
    def _():
        start_fetch(s + 1, other)

    @pl.when(s >= 2)
    def _():
        out_copy(0, slot).wait()

    mod = mod_ref[0:1, :]
    sh1, sc1, gt1, sh2, sc2, gt2 = [
        mod[:, i * D_MODEL:(i + 1) * D_MODEL] for i in range(6)]
    g_attn = v1024_ref[0:1, :]
    g_mlp = v1024_ref[1:2, :]

    def vec(i):
        return v256_ref[i:i + 1, :]

    lane = lax.broadcasted_iota(jnp.int32, (1, GROUP_W), 1)
    gmat32 = _group_ones(A_QK_DIM)
    gmat64 = _group_ones(A_V_DIM)
    lam, lam_init = _lambda(lam_ref)

    x = xbuf[slot]
    h = _rms(x, g_attn) * (1.0 + sc1) + sh1
    z = _dot(h.astype(BF16), w_in_ref[...])
    zq = z[:, 0:GROUP_W]
    zk = z[:, GROUP_W:2 * GROUP_W]
    zv = z[:, 2 * GROUP_W:3 * GROUP_W]
    zb = z[:, 3 * GROUP_W:5 * GROUP_W]
    zc = z[:, 5 * GROUP_W:7 * GROUP_W]
    zd = z[:, 7 * GROUP_W:8 * GROUP_W]

    q = _group_rms(zq, gmat32, A_QK_DIM, vec(_V_GQ))
    k = _group_rms(zk, gmat32, A_QK_DIM, vec(_V_GK))
    nk_ref[...] = k
    nv_ref[...] = zv
    q_s[...] = q * Q_SCALE
    k_s[...] = k.astype(BF16)
    v_s[...] = zv.astype(BF16)

    gb = jax.nn.gelu(zb, approximate=True)
    ub_s[...] = gb[:, 0:GROUP_W]
    vn_s[...] = _layer_norm(gb[:, GROUP_W:], vec(_V_GSG), vec(_V_BSG)).astype(BF16)

    zero_pad = jnp.zeros((CONV_PAD, GROUP_W), F32)
    hp_s[0:CONV_PAD, :] = zero_pad
    hp_s[CONV_PAD + L:2 * CONV_PAD + L, :] = zero_pad
    hp_s[CONV_PAD:CONV_PAD + L, :] = zc[:, 0:GROUP_W] * _sigmoid(zc[:, GROUP_W:])

    zcs = _dot(zd.astype(BF16), bdcs_ref[...])
    zc_s[...] = zcs[:, 0:GROUP_W].astype(BF16)
    zs_s[...] = zcs[:, GROUP_W:].astype(BF16)

    qf = q_s[...]
    o = jnp.zeros((L, GROUP_W), F32)
    for hh in range(A_HEADS):
        oh = _attention_head(qf, k_s[...], v_s[...], hh, lam, lane)
        hm = (lane >= hh * A_V_DIM) & (lane < (hh + 1) * A_V_DIM)
        o = jnp.where(hm, oh, o)
    o_a = _group_rms(o, gmat64, A_V_DIM, vec(_V_GHEAD)) * (1.0 - lam_init)
    mix_s[:, 0:GROUP_W] = o_a.astype(BF16)

    for c in range(L // CHUNK):
        crow = slice(c * CHUNK, (c + 1) * CHUNK)
        mixed = _spatial_gate(_dot(wsp_ref[...], vn_s[crow, :]), bsp_ref[...], lane)
        mix_s[crow, GROUP_W:2 * GROUP_W] = (ub_s[crow, :] * mixed).astype(BF16)

    yc = _layer_norm(_depthwise_conv(hp_s[...], wdw_ref, vec(_V_BDW), L),
                     vec(_V_GCONV), vec(_V_BCONV))
    mix_s[:, 2 * GROUP_W:3 * GROUP_W] = (yc * _sigmoid(yc)).astype(BF16)

    o_d = _dot(csl_ref[:, 0:L], zc_s[...]) + _dot(csl_ref[:, L:2 * L], zs_s[...])
    mix_s[:, 3 * GROUP_W:4 * GROUP_W] = o_d.astype(BF16)

    x1 = x + gt1 * _dot(mix_s[...], w_out_ref[...])
    h2 = (_rms(x1, g_mlp) * (1.0 + sc2) + sh2).astype(BF16)
    f = jnp.maximum(_dot(h2, w1_ref[...]), 0.0)
    f = (f * f).astype(BF16)
    obuf[slot] = x1 + gt2 * _dot(f, w2_ref[...])

    out_copy(s % n_seq, slot).start()

    @pl.when(s == n_items - 1)
    def _():
        out_copy(0, slot).wait()
        out_copy(0, other).wait()


def _context_trunk(x, mod_all, params, consts, weights0, weights_f32):
    n_seq, L, _ = x.shape
    n_items = DEPTH * n_seq
    assert n_seq >= 4
    bdcs, csl = consts

    def layer_spec(arr):
        nd = arr.ndim - 1
        return pl.BlockSpec((None,) + arr.shape[1:],
                            lambda s, _n=nd: (s // n_seq,) + (0,) * _n)

    def const_spec(arr):
        return pl.BlockSpec(arr.shape, lambda s, _n=arr.ndim: (0,) * _n,
                            pipeline_mode=pl.Buffered(1))

    def next_layer_slice(s):
        layer = s // n_seq
        more = layer + 1 < DEPTH
        return (jnp.minimum(layer + 1, DEPTH - 1), jnp.where(more, s % n_seq, n_seq - 1), 0)

    def slice_in_spec(w):
        return pl.BlockSpec((None, w.shape[1] // n_seq, w.shape[2]), next_layer_slice)

    def slice_out_spec(w):
        def index(s):
            layer, piece, _ = next_layer_slice(s)
            return (layer - 1, piece, 0)
        return pl.BlockSpec((None, w.shape[1] // n_seq, w.shape[2]), index)

    any_spec = pl.BlockSpec(memory_space=pl.ANY)
    in_specs = ([any_spec, layer_spec(mod_all)]
                + [layer_spec(p) for p in params]
                + [const_spec(bdcs), const_spec(csl)]
                + [any_spec] * len(weights0)
                + [slice_in_spec(w) for w in weights_f32])
    args = [x, mod_all, *params, bdcs, csl, *weights0, *weights_f32]
    kv_out = pl.BlockSpec((None, None, L, GROUP_W), lambda s: (s % n_seq, s // n_seq, 0, 0))
    out_specs = [any_spec, kv_out, kv_out] + [slice_out_spec(w) for w in weights_f32]
    out_shape = [jax.ShapeDtypeStruct(x.shape, F32)] + [
        jax.ShapeDtypeStruct((n_seq, DEPTH, L, GROUP_W), F32)] * 2 + [
        jax.ShapeDtypeStruct((DEPTH - 1,) + w.shape[1:], BF16) for w in weights_f32]
    scratch = [
        pltpu.VMEM((2, L, D_MODEL), F32),
        pltpu.VMEM((2, L, D_MODEL), F32),
        pltpu.VMEM((L, GROUP_W), F32),
        pltpu.VMEM((L, GROUP_W), BF16),
        pltpu.VMEM((L, GROUP_W), BF16),
        pltpu.VMEM((L, GROUP_W), F32),
        pltpu.VMEM((L, GROUP_W), BF16),
        pltpu.VMEM((L + 2 * CONV_PAD, GROUP_W), F32),
        pltpu.VMEM((L, GROUP_W), BF16),
        pltpu.VMEM((L, GROUP_W), BF16),
        pltpu.VMEM((L, D_MODEL), BF16),
    ] + [pltpu.VMEM((2,) + w.shape[1:], BF16) for w in weights_f32] + [
        pltpu.SemaphoreType.DMA((2,)),
        pltpu.SemaphoreType.DMA((2,)),
        pltpu.SemaphoreType.DMA((len(weights0),)),
    ]
    kern = functools.partial(_ctx_kernel, n_seq=n_seq, seq_len=L)
    return pl.pallas_call(
        kern,
        grid=(n_items,),
        in_specs=in_specs,
        out_specs=out_specs,
        out_shape=out_shape,
        scratch_shapes=scratch,
        compiler_params=pltpu.CompilerParams(
            dimension_semantics=("arbitrary",),
            vmem_limit_bytes=CTX_VMEM_LIMIT),
        name="context_trunk",
    )(*args)


def _latent_kernel(x_ref, mod_ref, v1024_ref, v256_ref, lam_ref, wsp_ref, bsp_ref, wdw_ref,
                   bdcs_ref, csl_ref, w_in_ref, w_out_ref, w1_ref, w2_ref,
                   ck_ref, cv_ref, cos_ref, sin_ref,
                   xo_ref,
                   q_s, k_s, v_s, ub_s, vn_s, hp_s, zc_s, zs_s, mix_s, *, seq_len, past_len):
    L = seq_len
    n_blk = L // ROW_BLOCK

    mod = mod_ref[pl.ds(1 + pl.program_id(0), 1), :]
    sh1, sc1, gt1, sh2, sc2, gt2 = [
        mod[:, i * D_MODEL:(i + 1) * D_MODEL] for i in range(6)]
    g_attn = v1024_ref[0:1, :]
    g_mlp = v1024_ref[1:2, :]

    def vec(i):
        return v256_ref[i:i + 1, :]

    lane = lax.broadcasted_iota(jnp.int32, (1, GROUP_W), 1)
    gmat32 = _group_ones(A_QK_DIM)
    gmat64 = _group_ones(A_V_DIM)
    lam, lam_init = _lambda(lam_ref)
    swap_lo = (lane // (A_QK_DIM // 4)) % 2 == 0

    zero_pad = jnp.zeros((CONV_PAD, GROUP_W), F32)
    hp_s[0:CONV_PAD, :] = zero_pad
    hp_s[CONV_PAD + L:2 * CONV_PAD + L, :] = zero_pad
    k_s[0:past_len, :] = ck_ref[...].astype(BF16)
    v_s[0:past_len, :] = cv_ref[...].astype(BF16)

    def phase1(r0):
        rows = pl.ds(r0, ROW_BLOCK)
        x = x_ref[0, rows, :]
        h = _rms(x, g_attn) * (1.0 + sc1) + sh1
        z = _dot(h.astype(BF16), w_in_ref[...])
        zq = z[:, 0:GROUP_W]
        zk = z[:, GROUP_W:2 * GROUP_W]
        zv = z[:, 2 * GROUP_W:3 * GROUP_W]
        zb = z[:, 3 * GROUP_W:5 * GROUP_W]
        zc = z[:, 5 * GROUP_W:7 * GROUP_W]
        zd = z[:, 7 * GROUP_W:8 * GROUP_W]

        cs = cos_ref[rows, :]
        sn = sin_ref[rows, :]

        def rope(t):
            swapped = jnp.where(
                swap_lo,
                pltpu.roll(t, GROUP_W - A_QK_DIM // 4, axis=1),
                pltpu.roll(t, A_QK_DIM // 4, axis=1))
            return t * cs + swapped * sn

        q = rope(_group_rms(zq, gmat32, A_QK_DIM, vec(_V_GQ)))
        k = rope(_group_rms(zk, gmat32, A_QK_DIM, vec(_V_GK)))
        q_s[rows, :] = q * Q_SCALE
        k_s[pl.ds(past_len + r0, ROW_BLOCK), :] = k.astype(BF16)
        v_s[pl.ds(past_len + r0, ROW_BLOCK), :] = zv.astype(BF16)

        gb = jax.nn.gelu(zb, approximate=True)
        ub_s[rows, :] = gb[:, 0:GROUP_W]
        vn_s[rows, :] = _layer_norm(
            gb[:, GROUP_W:], vec(_V_GSG), vec(_V_BSG)).astype(BF16)

        hp_s[pl.ds(CONV_PAD + r0, ROW_BLOCK), :] = (
            zc[:, 0:GROUP_W] * _sigmoid(zc[:, GROUP_W:]))

        zcs = _dot(zd.astype(BF16), bdcs_ref[...])
        zc_s[rows, :] = zcs[:, 0:GROUP_W].astype(BF16)
        zs_s[rows, :] = zcs[:, GROUP_W:].astype(BF16)

    def phase2(r0):
        rows = pl.ds(r0, ROW_BLOCK)

        qf = q_s[rows, :]

        def head(hh, o):
            oh = _attention_head(qf, k_s[...], v_s[...], hh, lam, lane)
            hm = (lane >= hh * A_V_DIM) & (lane < (hh + 1) * A_V_DIM)
            return jnp.where(hm, oh, o)

        o = lax.fori_loop(0, A_HEADS, head, jnp.zeros((ROW_BLOCK, GROUP_W), F32),
                          unroll=True)
        o_a = _group_rms(o, gmat64, A_V_DIM, vec(_V_GHEAD)) * (1.0 - lam_init)
        mix_s[rows, 0:GROUP_W] = o_a.astype(BF16)

        for c in range(ROW_BLOCK // CHUNK):
            crow = pl.ds(r0 + c * CHUNK, CHUNK)
            mixed = _spatial_gate(_dot(wsp_ref[...], vn_s[crow, :]), bsp_ref[...], lane)
            mix_s[crow, GROUP_W:2 * GROUP_W] = (ub_s[crow, :] * mixed).astype(BF16)

        win = hp_s[pl.ds(r0, ROW_BLOCK + 2 * CONV_PAD), :]
        yc = _layer_norm(_depthwise_conv(win, wdw_ref, vec(_V_BDW), ROW_BLOCK),
                         vec(_V_GCONV), vec(_V_BCONV))
        mix_s[rows, 2 * GROUP_W:3 * GROUP_W] = (yc * _sigmoid(yc)).astype(BF16)

        o_d = (_dot(csl_ref[rows, 0:L], zc_s[...])
               + _dot(csl_ref[rows, L:2 * L], zs_s[...]))
        mix_s[rows, 3 * GROUP_W:4 * GROUP_W] = o_d.astype(BF16)

        x = x_ref[0, rows, :]
        x1 = x + gt1 * _dot(mix_s[rows, :], w_out_ref[...])
        h2 = (_rms(x1, g_mlp) * (1.0 + sc2) + sh2).astype(BF16)
        f = jnp.maximum(_dot(h2, w1_ref[...]), 0.0)
        f = (f * f).astype(BF16)
        xo_ref[0, rows, :] = x1 + gt2 * _dot(f, w2_ref[...])

    def for_each_block(phase):
        def step(r, carry):
            phase(pl.multiple_of(r * ROW_BLOCK, ROW_BLOCK))
            return carry
        lax.fori_loop(0, n_blk, step, 0)

    for_each_block(phase1)
    for_each_block(phase2)


def _latent_layer(x, layer, mod_all, params, consts, weights, w_index,
                  cache_k, cache_v, cos_t, sin_t):
    n_seq, L, _ = x.shape
    past_len = cache_k.shape[2]
    bdcs, csl = consts
    single = dict(pipeline_mode=pl.Buffered(1))

    def layer_spec(arr, index=layer):
        nd = arr.ndim - 1
        return pl.BlockSpec((None,) + arr.shape[1:],
                            lambda b, _n=nd: (index,) + (0,) * _n, **single)

    def const_spec(arr):
        return pl.BlockSpec(arr.shape, lambda b, _n=arr.ndim: (0,) * _n, **single)

    x_spec = pl.BlockSpec((1, L, D_MODEL), lambda b: (b, 0, 0), **single)
    kv_spec = pl.BlockSpec((None, None, past_len, GROUP_W), lambda b: (b, layer, 0, 0))
    in_specs = ([x_spec, layer_spec(mod_all)]
                + [layer_spec(p) for p in params]
                + [const_spec(bdcs), const_spec(csl)]
                + [layer_spec(w, w_index) for w in weights]
                + [kv_spec, kv_spec, const_spec(cos_t), const_spec(sin_t)])
    args = [x, mod_all, *params, bdcs, csl, *weights, cache_k, cache_v, cos_t, sin_t]
    lk = past_len + L
    scratch = [
        pltpu.VMEM((L, GROUP_W), F32),
        pltpu.VMEM((lk, GROUP_W), BF16),
        pltpu.VMEM((lk, GROUP_W), BF16),
        pltpu.VMEM((L, GROUP_W), F32),
        pltpu.VMEM((L, GROUP_W), BF16),
        pltpu.VMEM((L + 2 * CONV_PAD, GROUP_W), F32),
        pltpu.VMEM((L, GROUP_W), BF16),
        pltpu.VMEM((L, GROUP_W), BF16),
        pltpu.VMEM((L, D_MODEL), BF16),
    ]
    kern = functools.partial(_latent_kernel, seq_len=L, past_len=past_len)
    return pl.pallas_call(
        kern,
        grid=(n_seq,),
        in_specs=in_specs,
        out_specs=x_spec,
        out_shape=jax.ShapeDtypeStruct(x.shape, F32),
        scratch_shapes=scratch,
        compiler_params=pltpu.CompilerParams(
            dimension_semantics=("arbitrary",),
            vmem_limit_bytes=VMEM_LIMIT),
        name=f"latent_layer{layer}",
    )(*args)


def _dft_tables(L):
    nc = GROUP_W // D_GROUPS
    cc = np.arange(nc)
    ang_c = 2.0 * np.pi * ((cc[:, None] * cc[None, :]) % nc) / nc
    bd_c = np.kron(np.eye(D_GROUPS), np.cos(ang_c) / math.sqrt(nc))
    bd_s = np.kron(np.eye(D_GROUPS), np.sin(ang_c) / math.sqrt(nc))
    bdcs = np.concatenate([bd_c, bd_s], axis=1)
    pp = np.arange(L)
    ang_l = 2.0 * np.pi * ((pp[:, None] * pp[None, :]) % L) / L
    csl = np.concatenate([np.cos(ang_l), -np.sin(ang_l)], axis=1) / math.sqrt(L)
    return jnp.asarray(bdcs, F32).astype(BF16), jnp.asarray(csl, F32).astype(BF16)


def _rope_tables(n_tokens):
    rows = n_tokens // GRID_W
    row = np.repeat(np.arange(rows, dtype=np.float64), GRID_W)
    col = np.tile(np.arange(GRID_W, dtype=np.float64), rows)
    nf = A_QK_DIM // 4
    inv = ROPE_BASE ** (-np.arange(nf, dtype=np.float64) / nf)
    ang = np.stack([row[:, None] * inv, col[:, None] * inv], axis=1)
    cos, sin = np.cos(ang), np.sin(ang)
    cos32 = np.stack([cos, cos], axis=2).reshape(n_tokens, A_QK_DIM)
    sin32 = np.stack([-sin, sin], axis=2).reshape(n_tokens, A_QK_DIM)
    reps = GROUP_W // A_QK_DIM
    return (jnp.asarray(np.tile(cos32, (1, reps)), F32),
            jnp.asarray(np.tile(sin32, (1, reps)), F32))


def kernel(x_prompt, x_sample, c, cache_k, cache_v, c_ctx, w_ada, b_ada, g_attn_norm, g_mlp_norm, w_in, g_q, g_k, lam_q1, lam_k1, lam_q2, lam_k2, g_head, g_sg, b_sg, w_spatial, b_spatial, w_dw, b_dw, g_conv, b_conv, w_out, w_ff1, w_ff2):
    n_p, l_p, _ = x_prompt.shape
    n_s, l_s, _ = x_sample.shape
    past_len = cache_k.shape[2]

    cond = jnp.concatenate(
        [c_ctx[None, :], c, jnp.zeros((ADA_ROWS - 1 - n_s, D_MODEL), F32)], axis=0)
    mod_all = _ada_modulation(cond, w_ada, b_ada)

    weights_f32 = (w_in, w_out, w_ff1, w_ff2)
    weights0 = tuple(w[0].astype(BF16) for w in weights_f32)

    bdcs, csl_p = _dft_tables(l_p)
    _, csl_s = _dft_tables(l_s)
    cos_t, sin_t = _rope_tables(l_s)
    ck4 = cache_k.reshape(n_s, DEPTH, past_len, GROUP_W)
    cv4 = cache_v.reshape(n_s, DEPTH, past_len, GROUP_W)

    v1024 = jnp.stack([g_attn_norm, g_mlp_norm], axis=1)
    rows = [jnp.tile(g_q, (1, GROUP_W // A_QK_DIM)), jnp.tile(g_k, (1, GROUP_W // A_QK_DIM)),
            jnp.tile(g_head, (1, GROUP_W // A_V_DIM)), g_sg, b_sg, b_dw, g_conv, b_conv]
    v256 = jnp.concatenate(
        [jnp.stack(rows, axis=1), jnp.zeros((DEPTH, 16 - len(rows), GROUP_W), F32)], axis=1)
    lam_init = jnp.asarray(
        [0.8 - 0.6 * math.exp(-0.3 * l) for l in range(DEPTH)], F32)
    lam5 = jnp.stack([lam_q1, lam_k1, lam_q2, lam_k2,
                      jnp.broadcast_to(lam_init[:, None], lam_q1.shape)], axis=1)
    lam5 = jnp.concatenate([lam5, jnp.zeros((DEPTH, 3, A_QK_DIM), F32)], axis=1)
    wsp = w_spatial.reshape(DEPTH, B_GROUPS * CHUNK, CHUNK).astype(BF16)
    bsp = jnp.repeat(jnp.swapaxes(b_spatial, 1, 2), GROUP_W // B_GROUPS, axis=2)
    wdw = jnp.concatenate([w_dw, jnp.zeros((DEPTH, 1, GROUP_W), F32)], axis=1)
    params = (v1024, v256, lam5, wsp, bsp, wdw)

    xp, new_k, new_v, *weights_rest = _context_trunk(
        x_prompt, mod_all, params, (bdcs, csl_p), weights0, weights_f32)
    xs = x_sample
    for l in range(DEPTH):
        if l == 0:
            weights, w_index = tuple(w[None] for w in weights0), 0
        else:
            weights, w_index = tuple(weights_rest), l - 1
        xs = _latent_layer(xs, l, mod_all, params, (bdcs, csl_s), weights, w_index,
                           ck4, cv4, cos_t, sin_t)

    new_k = new_k.reshape(n_p, DEPTH, l_p, A_HEADS, 2 * A_QK_DIM)
    new_v = new_v.reshape(n_p, DEPTH, l_p, A_HEADS, A_V_DIM)
    return (xp, xs, new_k, new_v)
```

```python
import functools
import math

import numpy as np
import jax
import jax.numpy as jnp
from jax import lax
from jax.experimental import pallas as pl
from jax.experimental.pallas import tpu as pltpu

F32 = jnp.float32
BF16 = jnp.bfloat16

D_MODEL = 1024
DEPTH = 4
GRID_W = 64
GROUP_W = 256
A_HEADS = 4
A_V_DIM = 64
A_QK_DIM = 32
B_GROUPS = 4
CHUNK = 128
C_KERNEL = 31
D_GROUPS = 4
D_FF = 4 * D_MODEL
ROPE_BASE = 10000.0
EPS = 1e-6
IN_W = 8 * GROUP_W
LOG2_E = 1.4426950408889634

ROW_BLOCK = 256
CONV_PAD = 16
ADA_ROWS = 8
ADA_COLS = 2048
CAST_STEPS = 8
VMEM_LIMIT = 58 * 1024 * 1024
BIG_VMEM_LIMIT = 62 * 1024 * 1024

_V_GQ, _V_GK, _V_GHEAD, _V_GSG, _V_BSG, _V_BDW, _V_GCONV, _V_BCONV = range(8)

Q_SCALE = A_QK_DIM ** -0.5 * LOG2_E


def _dot(a, b):
    return jnp.dot(a, b, preferred_element_type=F32)


def _sigmoid(x):
    return 1.0 / (1.0 + jnp.exp(-x))


def _rms(x, g):
    ms = jnp.mean(x * x, axis=-1, keepdims=True)
    return x * lax.rsqrt(ms + EPS) * g


def _layer_norm(x, g, b):
    mu = jnp.mean(x, axis=-1, keepdims=True)
    xc = x - mu
    var = jnp.mean(xc * xc, axis=-1, keepdims=True)
    return xc * lax.rsqrt(var + EPS) * g + b


def _group_rms(x, gmat, gsize, g):
    ms = _dot((x * x).astype(BF16), gmat) * (1.0 / gsize)
    return x * lax.rsqrt(ms + EPS) * g


def _group_ones(gsize):
    row_i = lax.broadcasted_iota(jnp.int32, (GROUP_W, GROUP_W), 0)
    col_i = lax.broadcasted_iota(jnp.int32, (GROUP_W, GROUP_W), 1)
    return jnp.where(row_i // gsize == col_i // gsize, 1.0, 0.0).astype(BF16)


def _lambda(lam_ref):
    lam_v = lam_ref[...]
    l1 = jnp.sum(lam_v[0:1] * lam_v[1:2], axis=-1, keepdims=True)
    l2 = jnp.sum(lam_v[2:3] * lam_v[3:4], axis=-1, keepdims=True)
    lam_init = lam_v[4:5, 0:1]
    return jnp.exp(l1) - jnp.exp(l2) + lam_init, lam_init


def _attention_head(qf, kb, vb, hh, lam, lane):
    outs = []
    for c in range(2):
        lo = hh * A_V_DIM + c * A_QK_DIM
        m = (lane >= lo) & (lane < lo + A_QK_DIM)
        qm = jnp.where(m, qf, 0.0).astype(BF16)
        s = lax.dot_general(qm, kb, (((1,), (1,)), ((), ())),
                            preferred_element_type=F32)
        e = jnp.exp2(s - jnp.max(s, axis=-1, keepdims=True))
        inv = 1.0 / jnp.sum(e, axis=-1, keepdims=True)
        outs.append(_dot(e.astype(BF16), vb) * inv)
    return outs[0] - lam * outs[1]


def _spatial_gate(res, bias, lane):
    mixed = bias
    for g in range(B_GROUPS):
        gm = (lane >= g * (GROUP_W // B_GROUPS)) & (lane < (g + 1) * (GROUP_W // B_GROUPS))
        mixed = mixed + jnp.where(gm, res[g * CHUNK:(g + 1) * CHUNK, :], 0.0)
    return mixed


def _depthwise_conv(win, wdw_ref, bias, n_rows):
    acc = jnp.zeros((n_rows, GROUP_W), F32) + bias
    for s_off in range(8):
        part = None
        for m_off in range(0, 2 * CONV_PAD, 8):
            j = m_off + s_off - 1
            if j < 0 or j >= C_KERNEL:
                continue
            term = win[m_off:m_off + n_rows + 8, :] * wdw_ref[j:j + 1, :]
            part = term if part is None else part + term
        acc = acc + part[s_off:s_off + n_rows, :]
    return acc


def _ada_kernel(cond_ref, w_ref, b_ref, o_ref):
    cnd = cond_ref[...]
    s = (cnd * _sigmoid(cnd)).astype(BF16)
    o_ref[0] = _dot(s, w_ref[0].astype(BF16)) + b_ref[0]


def _ada_modulation(cond, w_ada, b_ada):
    ncol = w_ada.shape[-1]
    return pl.pallas_call(
        _ada_kernel,
        grid=(DEPTH, ncol // ADA_COLS),
        in_specs=[
            pl.BlockSpec((ADA_ROWS, D_MODEL), lambda l, j: (0, 0)),
            pl.BlockSpec((1, D_MODEL, ADA_COLS), lambda l, j: (l, 0, j)),
            pl.BlockSpec((1, 1, ADA_COLS), lambda l, j: (l, 0, j)),
        ],
        out_specs=pl.BlockSpec((1, ADA_ROWS, ADA_COLS), lambda l, j: (l, 0, j)),
        out_shape=jax.ShapeDtypeStruct((DEPTH, ADA_ROWS, ncol), F32),
        compiler_params=pltpu.CompilerParams(
            dimension_semantics=("arbitrary", "arbitrary"),
            vmem_limit_bytes=VMEM_LIMIT),
        name="ada_modulation",
    )(cond, w_ada, b_ada.reshape(DEPTH, 1, ncol))


def _cast_kernel(*refs):
    n = len(refs) // 2
    for src, dst in zip(refs[:n], refs[n:]):
        dst[...] = src[...].astype(BF16)


def _cast_first_layer(weights_f32):
    def in_spec(w):
        return pl.BlockSpec((None, w.shape[1] // CAST_STEPS, w.shape[2]), lambda i: (0, i, 0))

    def out_spec(w):
        return pl.BlockSpec((w.shape[1] // CAST_STEPS, w.shape[2]), lambda i: (i, 0))

    return pl.pallas_call(
        _cast_kernel,
        grid=(CAST_STEPS,),
        in_specs=[in_spec(w) for w in weights_f32],
        out_specs=[out_spec(w) for w in weights_f32],
        out_shape=[jax.ShapeDtypeStruct(w.shape[1:], BF16) for w in weights_f32],
        compiler_params=pltpu.CompilerParams(
            dimension_semantics=("arbitrary",),
            vmem_limit_bytes=VMEM_LIMIT),
        name="cast_first_layer",
    )(*weights_f32)


def _ctx_kernel(xp_hbm, mod_ref, v1024_ref, v256_ref, lam_ref, wsp_ref, bsp_ref, wdw_ref,
                bdcs_ref, csl_ref,
                w_in0_hbm, w_out0_hbm, w1_0_hbm, w2_0_hbm,
                w_in_f32, w_out_f32, w1_f32, w2_f32,
                xo_hbm, nk_ref, nv_ref, w_in_o, w_out_o, w1_o, w2_o,
                xbuf, obuf, q_s, k_s, v_s, ub_s, vn_s, hp_s, zc_s, zs_s, mix_s,
                w_in_s, w_out_s, w1_s, w2_s,
                in_sem, out_sem, w_sem, *, n_seq, seq_len):
    L = seq_len
    n_items = DEPTH * n_seq
    s = pl.program_id(0)
    slot = s % 2
    other = 1 - slot

    w_cur = (s // n_seq) % 2
    w_nxt = 1 - w_cur
    chunk = s % n_seq
    layer0 = ((w_in0_hbm, w_in_s), (w_out0_hbm, w_out_s), (w1_0_hbm, w1_s), (w2_0_hbm, w2_s))

    @pl.when(s == 0)
    def _():
        copies = [pltpu.make_async_copy(src, dst.at[0], w_sem.at[i])
                  for i, (src, dst) in enumerate(layer0)]
        for cp in copies:
            cp.start()
        for cp in copies:
            cp.wait()

    for f32_ref, out_ref, dst in ((w_in_f32, w_in_o, w_in_s), (w_out_f32, w_out_o, w_out_s),
                                  (w1_f32, w1_o, w1_s), (w2_f32, w2_o, w2_s)):
        rows = f32_ref.shape[0]
        piece = f32_ref[...].astype(BF16)
        out_ref[...] = piece
        dst[w_nxt, pl.ds(pl.multiple_of(chunk * rows, rows), rows), :] = piece

    w_in_ref = w_in_s.at[w_cur]
    w_out_ref = w_out_s.at[w_cur]
    w1_ref = w1_s.at[w_cur]
    w2_ref = w2_s.at[w_cur]

    def fetch_copy(src_hbm, seq, dst_slot):
        return pltpu.make_async_copy(src_hbm.at[seq], xbuf.at[dst_slot], in_sem.at[dst_slot])

    def start_fetch(item, dst_slot):
        seq = item % n_seq

        @pl.when(item < n_seq)
        def _():
            fetch_copy(xp_hbm, seq, dst_slot).start()

        @pl.when(item >= n_seq)
        def _():
            fetch_copy(xo_hbm, seq, dst_slot).start()

    def out_copy(seq, src_slot):
        return pltpu.make_async_copy(obuf.at[src_slot], xo_hbm.at[seq], out_sem.at[src_slot])

    @pl.when(s == 0)
    def _():
        start_fetch(s, slot)

    fetch_copy(xp_hbm, 0, slot).wait()

    @pl.when(s + 1 < n_items)
    def _():
        start_fetch(s + 1, other)

    @pl.when(s >= 2)
    def _():
        out_copy(0, slot).wait()

    mod = mod_ref[0:1, :]
    sh1, sc1, gt1, sh2, sc2, gt2 = [
        mod[:, i * D_MODEL:(i + 1) * D_MODEL] for i in range(6)]
    g_attn = v1024_ref[0:1, :]
    g_mlp = v1024_ref[1:2, :]

    def vec(i):
        return v256_ref[i:i + 1, :]

    lane = lax.broadcasted_iota(jnp.int32, (1, GROUP_W), 1)
    gmat32 = _group_ones(A_QK_DIM)
    gmat64 = _group_ones(A_V_DIM)
    lam, lam_init = _lambda(lam_ref)

    x = xbuf[slot]
    h = _rms(x, g_attn) * (1.0 + sc1) + sh1
    z = _dot(h.astype(BF16), w_in_ref[...])
    zq = z[:, 0:GROUP_W]
    zk = z[:, GROUP_W:2 * GROUP_W]
    zv = z[:, 2 * GROUP_W:3 * GROUP_W]
    zb = z[:, 3 * GROUP_W:5 * GROUP_W]
    zc = z[:, 5 * GROUP_W:7 * GROUP_W]
    zd = z[:, 7 * GROUP_W:8 * GROUP_W]

    q = _group_rms(zq, gmat32, A_QK_DIM, vec(_V_GQ))
    k = _group_rms(zk, gmat32, A_QK_DIM, vec(_V_GK))
    nk_ref[...] = k
    nv_ref[...] = zv
    q_s[...] = q * Q_SCALE
    k_s[...] = k.astype(BF16)
    v_s[...] = zv.astype(BF16)

    gb = jax.nn.gelu(zb, approximate=True)
    ub_s[...] = gb[:, 0:GROUP_W]
    vn_s[...] = _layer_norm(gb[:, GROUP_W:], vec(_V_GSG), vec(_V_BSG)).astype(BF16)

    zero_pad = jnp.zeros((CONV_PAD, GROUP_W), F32)
    hp_s[0:CONV_PAD, :] = zero_pad
    hp_s[CONV_PAD + L:2 * CONV_PAD + L, :] = zero_pad
    hp_s[CONV_PAD:CONV_PAD + L, :] = zc[:, 0:GROUP_W] * _sigmoid(zc[:, GROUP_W:])

    zcs = _dot(zd.astype(BF16), bdcs_ref[...])
    zc_s[...] = zcs[:, 0:GROUP_W].astype(BF16)
    zs_s[...] = zcs[:, GROUP_W:].astype(BF16)

    qf = q_s[...]
    o = jnp.zeros((L, GROUP_W), F32)
    for hh in range(A_HEADS):
        oh = _attention_head(qf, k_s[...], v_s[...], hh, lam, lane)
        hm = (lane >= hh * A_V_DIM) & (lane < (hh + 1) * A_V_DIM)
        o = jnp.where(hm, oh, o)
    o_a = _group_rms(o, gmat64, A_V_DIM, vec(_V_GHEAD)) * (1.0 - lam_init)
    mix_s[:, 0:GROUP_W] = o_a.astype(BF16)

    for c in range(L // CHUNK):
        crow = slice(c * CHUNK, (c + 1) * CHUNK)
        mixed = _spatial_gate(_dot(wsp_ref[...], vn_s[crow, :]), bsp_ref[...], lane)
        mix_s[crow, GROUP_W:2 * GROUP_W] = (ub_s[crow, :] * mixed).astype(BF16)

    yc = _layer_norm(_depthwise_conv(hp_s[...], wdw_ref, vec(_V_BDW), L),
                     vec(_V_GCONV), vec(_V_BCONV))
    mix_s[:, 2 * GROUP_W:3 * GROUP_W] = (yc * _sigmoid(yc)).astype(BF16)

    o_d = _dot(csl_ref[:, 0:L], zc_s[...]) + _dot(csl_ref[:, L:2 * L], zs_s[...])
    mix_s[:, 3 * GROUP_W:4 * GROUP_W] = o_d.astype(BF16)

    x1 = x + gt1 * _dot(mix_s[...], w_out_ref[...])
    h2 = (_rms(x1, g_mlp) * (1.0 + sc2) + sh2).astype(BF16)
    f = jnp.maximum(_dot(h2, w1_ref[...]), 0.0)
    f = (f * f).astype(BF16)
    obuf[slot] = x1 + gt2 * _dot(f, w2_ref[...])

    out_copy(s % n_seq, slot).start()

    @pl.when(s == n_items - 1)
    def _():
        out_copy(0, slot).wait()
        out_copy(0, other).wait()


def _context_trunk(x, mod_all, params, consts, weights0, weights_f32):
    n_seq, L, _ = x.shape
    n_items = DEPTH * n_seq
    assert n_seq >= 4
    bdcs, csl = consts

    def layer_spec(arr):
        nd = arr.ndim - 1
        return pl.BlockSpec((None,) + arr.shape[1:],
                            lambda s, _n=nd: (s // n_seq,) + (0,) * _n)

    def const_spec(arr):
        return pl.BlockSpec(arr.shape, lambda s, _n=arr.ndim: (0,) * _n,
                            pipeline_mode=pl.Buffered(1))

    def next_layer_slice(s):
        layer = s // n_seq
        more = layer + 1 < DEPTH
        return (jnp.minimum(layer + 1, DEPTH - 1), jnp.where(more, s % n_seq, n_seq - 1), 0)

    def slice_in_spec(w):
        return pl.BlockSpec((None, w.shape[1] // n_seq, w.shape[2]), next_layer_slice)

    def slice_out_spec(w):
        def index(s):
            layer, piece, _ = next_layer_slice(s)
            return (layer - 1, piece, 0)
        return pl.BlockSpec((None, w.shape[1] // n_seq, w.shape[2]), index)

    any_spec = pl.BlockSpec(memory_space=pl.ANY)
    in_specs = ([any_spec, layer_spec(mod_all)]
                + [layer_spec(p) for p in params]
                + [const_spec(bdcs), const_spec(csl)]
                + [any_spec] * len(weights0)
                + [slice_in_spec(w) for w in weights_f32])
    args = [x, mod_all, *params, bdcs, csl, *weights0, *weights_f32]
    kv_out = pl.BlockSpec((None, None, L, GROUP_W), lambda s: (s % n_seq, s // n_seq, 0, 0))
    out_specs = [any_spec, kv_out, kv_out] + [slice_out_spec(w) for w in weights_f32]
    out_shape = [jax.ShapeDtypeStruct(x.shape, F32)] + [
        jax.ShapeDtypeStruct((n_seq, DEPTH, L, GROUP_W), F32)] * 2 + [
        jax.ShapeDtypeStruct((DEPTH - 1,) + w.shape[1:], BF16) for w in weights_f32]
    scratch = [
        pltpu.VMEM((2, L, D_MODEL), F32),
        pltpu.VMEM((2, L, D_MODEL), F32),
        pltpu.VMEM((L, GROUP_W), F32),
        pltpu.VMEM((L, GROUP_W), BF16),
        pltpu.VMEM((L, GROUP_W), BF16),
        pltpu.VMEM((L, GROUP_W), F32),
        pltpu.VMEM((L, GROUP_W), BF16),
        pltpu.VMEM((L + 2 * CONV_PAD, GROUP_W), F32),
        pltpu.VMEM((L, GROUP_W), BF16),
        pltpu.VMEM((L, GROUP_W), BF16),
        pltpu.VMEM((L, D_MODEL), BF16),
    ] + [pltpu.VMEM((2,) + w.shape[1:], BF16) for w in weights_f32] + [
        pltpu.SemaphoreType.DMA((2,)),
        pltpu.SemaphoreType.DMA((2,)),
        pltpu.SemaphoreType.DMA((len(weights0),)),
    ]
    kern = functools.partial(_ctx_kernel, n_seq=n_seq, seq_len=L)
    return pl.pallas_call(
        kern,
        grid=(n_items,),
        in_specs=in_specs,
        out_specs=out_specs,
        out_shape=out_shape,
        scratch_shapes=scratch,
        compiler_params=pltpu.CompilerParams(
            dimension_semantics=("arbitrary",),
            vmem_limit_bytes=BIG_VMEM_LIMIT),
        name="context_trunk",
    )(*args)


def _latent_kernel(x_ref, mod_ref, v1024_ref, v256_ref, lam_ref, wsp_ref, bsp_ref, wdw_ref,
                   bdcs_ref, csl_ref, w_in_ref, w_out_ref, w1_ref, w2_ref,
                   ck_ref, cv_ref, cos_ref, sin_ref,
                   xo_ref,
                   q_s, k_s, v_s, ub_s, vn_s, hp_s, zc_s, zs_s, mix_s, *, seq_len, past_len):
    L = seq_len
    n_blk = L // ROW_BLOCK

    mod = mod_ref[pl.ds(1 + pl.program_id(0), 1), :]
    sh1, sc1, gt1, sh2, sc2, gt2 = [
        mod[:, i * D_MODEL:(i + 1) * D_MODEL] for i in range(6)]
    g_attn = v1024_ref[0:1, :]
    g_mlp = v1024_ref[1:2, :]

    def vec(i):
        return v256_ref[i:i + 1, :]

    lane = lax.broadcasted_iota(jnp.int32, (1, GROUP_W), 1)
    gmat32 = _group_ones(A_QK_DIM)
    gmat64 = _group_ones(A_V_DIM)
    lam, lam_init = _lambda(lam_ref)
    swap_lo = (lane // (A_QK_DIM // 4)) % 2 == 0

    zero_pad = jnp.zeros((CONV_PAD, GROUP_W), F32)
    hp_s[0:CONV_PAD, :] = zero_pad
    hp_s[CONV_PAD + L:2 * CONV_PAD + L, :] = zero_pad
    k_s[0:past_len, :] = ck_ref[...].astype(BF16)
    v_s[0:past_len, :] = cv_ref[...].astype(BF16)

    def phase1(r0):
        rows = pl.ds(r0, ROW_BLOCK)
        x = x_ref[0, rows, :]
        h = _rms(x, g_attn) * (1.0 + sc1) + sh1
        z = _dot(h.astype(BF16), w_in_ref[...])
        zq = z[:, 0:GROUP_W]
        zk = z[:, GROUP_W:2 * GROUP_W]
        zv = z[:, 2 * GROUP_W:3 * GROUP_W]
        zb = z[:, 3 * GROUP_W:5 * GROUP_W]
        zc = z[:, 5 * GROUP_W:7 * GROUP_W]
        zd = z[:, 7 * GROUP_W:8 * GROUP_W]

        cs = cos_ref[rows, :]
        sn = sin_ref[rows, :]

        def rope(t):
            swapped = jnp.where(
                swap_lo,
                pltpu.roll(t, GROUP_W - A_QK_DIM // 4, axis=1),
                pltpu.roll(t, A_QK_DIM // 4, axis=1))
            return t * cs + swapped * sn

        q = rope(_group_rms(zq, gmat32, A_QK_DIM, vec(_V_GQ)))
        k = rope(_group_rms(zk, gmat32, A_QK_DIM, vec(_V_GK)))
        q_s[rows, :] = q * Q_SCALE
        k_s[pl.ds(past_len + r0, ROW_BLOCK), :] = k.astype(BF16)
        v_s[pl.ds(past_len + r0, ROW_BLOCK), :] = zv.astype(BF16)

        gb = jax.nn.gelu(zb, approximate=True)
        ub_s[rows, :] = gb[:, 0:GROUP_W]
        vn_s[rows, :] = _layer_norm(
            gb[:, GROUP_W:], vec(_V_GSG), vec(_V_BSG)).astype(BF16)

        hp_s[pl.ds(CONV_PAD + r0, ROW_BLOCK), :] = (
            zc[:, 0:GROUP_W] * _sigmoid(zc[:, GROUP_W:]))

        zcs = _dot(zd.astype(BF16), bdcs_ref[...])
        zc_s[rows, :] = zcs[:, 0:GROUP_W].astype(BF16)
        zs_s[rows, :] = zcs[:, GROUP_W:].astype(BF16)

    def phase2(r0):
        rows = pl.ds(r0, ROW_BLOCK)

        qf = q_s[rows, :]

        def head(hh, o):
            oh = _attention_head(qf, k_s[...], v_s[...], hh, lam, lane)
            hm = (lane >= hh * A_V_DIM) & (lane < (hh + 1) * A_V_DIM)
            return jnp.where(hm, oh, o)

        o = lax.fori_loop(0, A_HEADS, head, jnp.zeros((ROW_BLOCK, GROUP_W), F32),
                          unroll=True)
        o_a = _group_rms(o, gmat64, A_V_DIM, vec(_V_GHEAD)) * (1.0 - lam_init)
        mix_s[rows, 0:GROUP_W] = o_a.astype(BF16)

        for c in range(ROW_BLOCK // CHUNK):
            crow = pl.ds(r0 + c * CHUNK, CHUNK)
            mixed = _spatial_gate(_dot(wsp_ref[...], vn_s[crow, :]), bsp_ref[...], lane)
            mix_s[crow, GROUP_W:2 * GROUP_W] = (ub_s[crow, :] * mixed).astype(BF16)

        win = hp_s[pl.ds(r0, ROW_BLOCK + 2 * CONV_PAD), :]
        yc = _layer_norm(_depthwise_conv(win, wdw_ref, vec(_V_BDW), ROW_BLOCK),
                         vec(_V_GCONV), vec(_V_BCONV))
        mix_s[rows, 2 * GROUP_W:3 * GROUP_W] = (yc * _sigmoid(yc)).astype(BF16)

        o_d = (_dot(csl_ref[rows, 0:L], zc_s[...])
               + _dot(csl_ref[rows, L:2 * L], zs_s[...]))
        mix_s[rows, 3 * GROUP_W:4 * GROUP_W] = o_d.astype(BF16)

        x = x_ref[0, rows, :]
        x1 = x + gt1 * _dot(mix_s[rows, :], w_out_ref[...])
        h2 = (_rms(x1, g_mlp) * (1.0 + sc2) + sh2).astype(BF16)
        f = jnp.maximum(_dot(h2, w1_ref[...]), 0.0)
        f = (f * f).astype(BF16)
        xo_ref[0, rows, :] = x1 + gt2 * _dot(f, w2_ref[...])

    def for_each_block(phase):
        def step(r, carry):
            phase(pl.multiple_of(r * ROW_BLOCK, ROW_BLOCK))
            return carry
        lax.fori_loop(0, n_blk, step, 0)

    for_each_block(phase1)
    for_each_block(phase2)


def _latent_layer(x, layer, mod_all, params, consts, weights, w_index,
                  cache_k, cache_v, cos_t, sin_t):
    n_seq, L, _ = x.shape
    past_len = cache_k.shape[2]
    bdcs, csl = consts
    single = dict(pipeline_mode=pl.Buffered(1))

    def layer_spec(arr, index=layer):
        nd = arr.ndim - 1
        return pl.BlockSpec((None,) + arr.shape[1:],
                            lambda b, _n=nd: (index,) + (0,) * _n, **single)

    def const_spec(arr):
        return pl.BlockSpec(arr.shape, lambda b, _n=arr.ndim: (0,) * _n, **single)

    x_spec = pl.BlockSpec((1, L, D_MODEL), lambda b: (b, 0, 0))
    kv_spec = pl.BlockSpec((None, None, past_len, GROUP_W), lambda b: (b, layer, 0, 0))
    in_specs = ([x_spec, layer_spec(mod_all)]
                + [layer_spec(p) for p in params]
                + [const_spec(bdcs), const_spec(csl)]
                + [layer_spec(w, w_index) for w in weights]
                + [kv_spec, kv_spec, const_spec(cos_t), const_spec(sin_t)])
    args = [x, mod_all, *params, bdcs, csl, *weights, cache_k, cache_v, cos_t, sin_t]
    lk = past_len + L
    scratch = [
        pltpu.VMEM((L, GROUP_W), F32),
        pltpu.VMEM((lk, GROUP_W), BF16),
        pltpu.VMEM((lk, GROUP_W), BF16),
        pltpu.VMEM((L, GROUP_W), F32),
        pltpu.VMEM((L, GROUP_W), BF16),
        pltpu.VMEM((L + 2 * CONV_PAD, GROUP_W), F32),
        pltpu.VMEM((L, GROUP_W), BF16),
        pltpu.VMEM((L, GROUP_W), BF16),
        pltpu.VMEM((L, D_MODEL), BF16),
    ]
    kern = functools.partial(_latent_kernel, seq_len=L, past_len=past_len)
    return pl.pallas_call(
        kern,
        grid=(n_seq,),
        in_specs=in_specs,
        out_specs=x_spec,
        out_shape=jax.ShapeDtypeStruct(x.shape, F32),
        scratch_shapes=scratch,
        compiler_params=pltpu.CompilerParams(
            dimension_semantics=("arbitrary",),
            vmem_limit_bytes=BIG_VMEM_LIMIT),
        name=f"latent_layer{layer}",
    )(*args)


def _dft_tables(L):
    nc = GROUP_W // D_GROUPS
    cc = np.arange(nc)
    ang_c = 2.0 * np.pi * ((cc[:, None] * cc[None, :]) % nc) / nc
    bd_c = np.kron(np.eye(D_GROUPS), np.cos(ang_c) / math.sqrt(nc))
    bd_s = np.kron(np.eye(D_GROUPS), np.sin(ang_c) / math.sqrt(nc))
    bdcs = np.concatenate([bd_c, bd_s], axis=1)
    pp = np.arange(L)
    ang_l = 2.0 * np.pi * ((pp[:, None] * pp[None, :]) % L) / L
    csl = np.concatenate([np.cos(ang_l), -np.sin(ang_l)], axis=1) / math.sqrt(L)
    return jnp.asarray(bdcs, F32).astype(BF16), jnp.asarray(csl, F32).astype(BF16)


def _rope_tables(n_tokens):
    rows = n_tokens // GRID_W
    row = np.repeat(np.arange(rows, dtype=np.float64), GRID_W)
    col = np.tile(np.arange(GRID_W, dtype=np.float64), rows)
    nf = A_QK_DIM // 4
    inv = ROPE_BASE ** (-np.arange(nf, dtype=np.float64) / nf)
    ang = np.stack([row[:, None] * inv, col[:, None] * inv], axis=1)
    cos, sin = np.cos(ang), np.sin(ang)
    cos32 = np.stack([cos, cos], axis=2).reshape(n_tokens, A_QK_DIM)
    sin32 = np.stack([-sin, sin], axis=2).reshape(n_tokens, A_QK_DIM)
    reps = GROUP_W // A_QK_DIM
    return (jnp.asarray(np.tile(cos32, (1, reps)), F32),
            jnp.asarray(np.tile(sin32, (1, reps)), F32))


def kernel(x_prompt, x_sample, c, cache_k, cache_v, c_ctx, w_ada, b_ada, g_attn_norm, g_mlp_norm, w_in, g_q, g_k, lam_q1, lam_k1, lam_q2, lam_k2, g_head, g_sg, b_sg, w_spatial, b_spatial, w_dw, b_dw, g_conv, b_conv, w_out, w_ff1, w_ff2):
    n_p, l_p, _ = x_prompt.shape
    n_s, l_s, _ = x_sample.shape
    past_len = cache_k.shape[2]

    cond = jnp.concatenate(
        [c_ctx[None, :], c, jnp.zeros((ADA_ROWS - 1 - n_s, D_MODEL), F32)], axis=0)
    mod_all = _ada_modulation(cond, w_ada, b_ada)

    weights_f32 = (w_in, w_out, w_ff1, w_ff2)
    weights0 = _cast_first_layer(weights_f32)

    bdcs, csl_p = _dft_tables(l_p)
    _, csl_s = _dft_tables(l_s)
    cos_t, sin_t = _rope_tables(l_s)
    ck4 = cache_k.reshape(n_s, DEPTH, past_len, GROUP_W)
    cv4 = cache_v.reshape(n_s, DEPTH, past_len, GROUP_W)

    v1024 = jnp.stack([g_attn_norm, g_mlp_norm], axis=1)
    rows = [jnp.tile(g_q, (1, GROUP_W // A_QK_DIM)), jnp.tile(g_k, (1, GROUP_W // A_QK_DIM)),
            jnp.tile(g_head, (1, GROUP_W // A_V_DIM)), g_sg, b_sg, b_dw, g_conv, b_conv]
    v256 = jnp.concatenate(
        [jnp.stack(rows, axis=1), jnp.zeros((DEPTH, 16 - len(rows), GROUP_W), F32)], axis=1)
    lam_init = jnp.asarray(
        [0.8 - 0.6 * math.exp(-0.3 * l) for l in range(DEPTH)], F32)
    lam5 = jnp.stack([lam_q1, lam_k1, lam_q2, lam_k2,
                      jnp.broadcast_to(lam_init[:, None], lam_q1.shape)], axis=1)
    lam5 = jnp.concatenate([lam5, jnp.zeros((DEPTH, 3, A_QK_DIM), F32)], axis=1)
    wsp = w_spatial.reshape(DEPTH, B_GROUPS * CHUNK, CHUNK).astype(BF16)
    bsp = jnp.repeat(jnp.swapaxes(b_spatial, 1, 2), GROUP_W // B_GROUPS, axis=2)
    wdw = jnp.concatenate([w_dw, jnp.zeros((DEPTH, 1, GROUP_W), F32)], axis=1)
    params = (v1024, v256, lam5, wsp, bsp, wdw)

    xp, new_k, new_v, *weights_rest = _context_trunk(
        x_prompt, mod_all, params, (bdcs, csl_p), weights0, weights_f32)
    xs = x_sample
    for l in range(DEPTH):
        if l == 0:
            weights, w_index = tuple(w[None] for w in weights0), 0
        else:
            weights, w_index = tuple(weights_rest), l - 1
        xs = _latent_layer(xs, l, mod_all, params, (bdcs, csl_s), weights, w_index,
                           ck4, cv4, cos_t, sin_t)

    new_k = new_k.reshape(n_p, DEPTH, l_p, A_HEADS, 2 * A_QK_DIM)
    new_v = new_v.reshape(n_p, DEPTH, l_p, A_HEADS, A_V_DIM)
    return (xp, xs, new_k, new_v)
```

```python
import functools
import math

import numpy as np
import jax
import jax.numpy as jnp
from jax import lax
from jax.experimental import pallas as pl
from jax.experimental.pallas import tpu as pltpu

F32 = jnp.float32
BF16 = jnp.bfloat16

D_MODEL = 1024
DEPTH = 4
GRID_W = 64
GROUP_W = 256
A_HEADS = 4
A_V_DIM = 64
A_QK_DIM = 32
B_GROUPS = 4
CHUNK = 128
C_KERNEL = 31
D_GROUPS = 4
D_FF = 4 * D_MODEL
ROPE_BASE = 10000.0
EPS = 1e-6
IN_W = 8 * GROUP_W
LOG2_E = 1.4426950408889634

ROW_BLOCK = 256
CONV_PAD = 16
ADA_ROWS = 8
ADA_COLS = 2048
CAST_STEPS = 8
VMEM_LIMIT = 58 * 1024 * 1024
BIG_VMEM_LIMIT = 62 * 1024 * 1024

_V_GQ, _V_GK, _V_GHEAD, _V_GSG, _V_BSG, _V_BDW, _V_GCONV, _V_BCONV = range(8)

Q_SCALE = A_QK_DIM ** -0.5 * LOG2_E


def _dot(a, b):
    return jnp.dot(a, b, preferred_element_type=F32)


def _sigmoid(x):
    return 1.0 / (1.0 + jnp.exp(-x))


def _rms(x, g):
    ms = jnp.mean(x * x, axis=-1, keepdims=True)
    return x * lax.rsqrt(ms + EPS) * g


def _layer_norm(x, g, b):
    mu = jnp.mean(x, axis=-1, keepdims=True)
    xc = x - mu
    var = jnp.mean(xc * xc, axis=-1, keepdims=True)
    return xc * lax.rsqrt(var + EPS) * g + b


def _group_rms(x, gmat, g):
    ms = _dot((x * x).astype(BF16), gmat)
    return x * lax.rsqrt(ms + EPS) * g


def _group_mean_matrix(gsize):
    row_i = lax.broadcasted_iota(jnp.int32, (GROUP_W, GROUP_W), 0)
    col_i = lax.broadcasted_iota(jnp.int32, (GROUP_W, GROUP_W), 1)
    return jnp.where(row_i // gsize == col_i // gsize, 1.0 / gsize, 0.0).astype(BF16)


def _lambda(lam_ref):
    lam_v = lam_ref[...]
    l1 = jnp.sum(lam_v[0:1] * lam_v[1:2], axis=-1, keepdims=True)
    l2 = jnp.sum(lam_v[2:3] * lam_v[3:4], axis=-1, keepdims=True)
    lam_init = lam_v[4:5, 0:1]
    return jnp.exp(l1) - jnp.exp(l2) + lam_init, lam_init


def _attention_head(qf, kb, vb, hh, lam, lane):
    outs = []
    for c in range(2):
        lo = hh * A_V_DIM + c * A_QK_DIM
        m = (lane >= lo) & (lane < lo + A_QK_DIM)
        qm = jnp.where(m, qf, 0.0).astype(BF16)
        s = lax.dot_general(qm, kb, (((1,), (1,)), ((), ())),
                            preferred_element_type=F32)
        e = jnp.exp2(s - jnp.max(s, axis=-1, keepdims=True))
        inv = 1.0 / jnp.sum(e, axis=-1, keepdims=True)
        outs.append(_dot(e.astype(BF16), vb) * inv)
    return outs[0] - lam * outs[1]


def _spatial_gate(res, bias, lane):
    mixed = bias
    for g in range(B_GROUPS):
        gm = (lane >= g * (GROUP_W // B_GROUPS)) & (lane < (g + 1) * (GROUP_W // B_GROUPS))
        mixed = mixed + jnp.where(gm, res[g * CHUNK:(g + 1) * CHUNK, :], 0.0)
    return mixed


def _depthwise_conv(win, wdw_ref, bias, n_rows):
    acc = jnp.zeros((n_rows, GROUP_W), F32) + bias
    for s_off in range(8):
        part = None
        for m_off in range(0, 2 * CONV_PAD, 8):
            j = m_off + s_off - 1
            if j < 0 or j >= C_KERNEL:
                continue
            term = win[m_off:m_off + n_rows + 8, :] * wdw_ref[j:j + 1, :]
            part = term if part is None else part + term
        acc = acc + part[s_off:s_off + n_rows, :]
    return acc


def _ada_kernel(cond_ref, w_ref, b_ref, o_ref):
    cnd = cond_ref[...]
    s = (cnd * _sigmoid(cnd)).astype(BF16)
    o_ref[0] = _dot(s, w_ref[0].astype(BF16)) + b_ref[0]


def _ada_modulation(cond, w_ada, b_ada):
    ncol = w_ada.shape[-1]
    return pl.pallas_call(
        _ada_kernel,
        grid=(DEPTH, ncol // ADA_COLS),
        in_specs=[
            pl.BlockSpec((ADA_ROWS, D_MODEL), lambda l, j: (0, 0)),
            pl.BlockSpec((1, D_MODEL, ADA_COLS), lambda l, j: (l, 0, j)),
            pl.BlockSpec((1, 1, ADA_COLS), lambda l, j: (l, 0, j)),
        ],
        out_specs=pl.BlockSpec((1, ADA_ROWS, ADA_COLS), lambda l, j: (l, 0, j)),
        out_shape=jax.ShapeDtypeStruct((DEPTH, ADA_ROWS, ncol), F32),
        compiler_params=pltpu.CompilerParams(
            dimension_semantics=("arbitrary", "arbitrary"),
            vmem_limit_bytes=VMEM_LIMIT),
        name="ada_modulation",
    )(cond, w_ada, b_ada.reshape(DEPTH, 1, ncol))


def _cast_kernel(*refs):
    n = len(refs) // 2
    for src, dst in zip(refs[:n], refs[n:]):
        dst[...] = src[...].astype(BF16)


def _cast_first_layer(weights_f32):
    def in_spec(w):
        return pl.BlockSpec((None, w.shape[1] // CAST_STEPS, w.shape[2]), lambda i: (0, i, 0))

    def out_spec(w):
        return pl.BlockSpec((w.shape[1] // CAST_STEPS, w.shape[2]), lambda i: (i, 0))

    return pl.pallas_call(
        _cast_kernel,
        grid=(CAST_STEPS,),
        in_specs=[in_spec(w) for w in weights_f32],
        out_specs=[out_spec(w) for w in weights_f32],
        out_shape=[jax.ShapeDtypeStruct(w.shape[1:], BF16) for w in weights_f32],
        compiler_params=pltpu.CompilerParams(
            dimension_semantics=("arbitrary",),
            vmem_limit_bytes=VMEM_LIMIT),
        name="cast_first_layer",
    )(*weights_f32)


def _ctx_kernel(xp_hbm, mod_ref, v1024_ref, v256_ref, lam_ref, wsp_ref, bsp_ref, wdw_ref,
                bdcs_ref, csl_ref,
                w_in0_hbm, w_out0_hbm, w1_0_hbm, w2_0_hbm,
                w_in_f32, w_out_f32, w1_f32, w2_f32,
                xo_hbm, nk_ref, nv_ref, w_in_o, w_out_o, w1_o, w2_o,
                xbuf, obuf, q_s, k_s, v_s, ub_s, vn_s, hp_s, zc_s, zs_s, mix_s,
                w_in_s, w_out_s, w1_s, w2_s,
                in_sem, out_sem, w_sem, *, n_seq, seq_len):
    L = seq_len
    n_items = DEPTH * n_seq
    s = pl.program_id(0)
    slot = s % 2
    other = 1 - slot

    w_cur = (s // n_seq) % 2
    w_nxt = 1 - w_cur
    chunk = s % n_seq
    layer0 = ((w_in0_hbm, w_in_s), (w_out0_hbm, w_out_s), (w1_0_hbm, w1_s), (w2_0_hbm, w2_s))

    @pl.when(s == 0)
    def _():
        copies = [pltpu.make_async_copy(src, dst.at[0], w_sem.at[i])
                  for i, (src, dst) in enumerate(layer0)]
        for cp in copies:
            cp.start()
        for cp in copies:
            cp.wait()

    for f32_ref, out_ref, dst in ((w_in_f32, w_in_o, w_in_s), (w_out_f32, w_out_o, w_out_s),
                                  (w1_f32, w1_o, w1_s), (w2_f32, w2_o, w2_s)):
        rows = f32_ref.shape[0]
        piece = f32_ref[...].astype(BF16)
        out_ref[...] = piece
        dst[w_nxt, pl.ds(pl.multiple_of(chunk * rows, rows), rows), :] = piece

    w_in_ref = w_in_s.at[w_cur]
    w_out_ref = w_out_s.at[w_cur]
    w1_ref = w1_s.at[w_cur]
    w2_ref = w2_s.at[w_cur]

    def fetch_copy(src_hbm, seq, dst_slot):
        return pltpu.make_async_copy(src_hbm.at[seq], xbuf.at[dst_slot], in_sem.at[dst_slot])

    def start_fetch(item, dst_slot):
        seq = item % n_seq

        @pl.when(item < n_seq)
        def _():
            fetch_copy(xp_hbm, seq, dst_slot).start()

        @pl.when(item >= n_seq)
        def _():
            fetch_copy(xo_hbm, seq, dst_slot).start()

    def out_copy(seq, src_slot):
        return pltpu.make_async_copy(obuf.at[src_slot], xo_hbm.at[seq], out_sem.at[src_slot])

    @pl.when(s == 0)
    def _():
        start_fetch(s, slot)

    fetch_copy(xp_hbm, 0, slot).wait()

    @pl.when(s + 1 < n_items)
    def _():
        start_fetch(s + 1, other)

    @pl.when(s >= 2)
    def _():
        out_copy(0, slot).wait()

    mod = mod_ref[0:1, :]
    sh1, sc1, gt1, sh2, sc2, gt2 = [
        mod[:, i * D_MODEL:(i + 1) * D_MODEL] for i in range(6)]
    g_attn = v1024_ref[0:1, :]
    g_mlp = v1024_ref[1:2, :]

    def vec(i):
        return v256_ref[i:i + 1, :]

    lane = lax.broadcasted_iota(jnp.int32, (1, GROUP_W), 1)
    gmat32 = _group_mean_matrix(A_QK_DIM)
    gmat64 = _group_mean_matrix(A_V_DIM)
    lam, lam_init = _lambda(lam_ref)

    x = xbuf[slot]
    h = _rms(x, g_attn) * (1.0 + sc1) + sh1
    z = _dot(h.astype(BF16), w_in_ref[...])
    zq = z[:, 0:GROUP_W]
    zk = z[:, GROUP_W:2 * GROUP_W]
    zv = z[:, 2 * GROUP_W:3 * GROUP_W]
    zb = z[:, 3 * GROUP_W:5 * GROUP_W]
    zc = z[:, 5 * GROUP_W:7 * GROUP_W]
    zd = z[:, 7 * GROUP_W:8 * GROUP_W]

    q = _group_rms(zq, gmat32, vec(_V_GQ))
    k = _group_rms(zk, gmat32, vec(_V_GK))
    nk_ref[...] = k
    nv_ref[...] = zv
    q_s[...] = q * Q_SCALE
    k_s[...] = k.astype(BF16)
    v_s[...] = zv.astype(BF16)

    gb = jax.nn.gelu(zb, approximate=True)
    ub_s[...] = gb[:, 0:GROUP_W]
    vn_s[...] = _layer_norm(gb[:, GROUP_W:], vec(_V_GSG), vec(_V_BSG)).astype(BF16)

    zero_pad = jnp.zeros((CONV_PAD, GROUP_W), F32)
    hp_s[0:CONV_PAD, :] = zero_pad
    hp_s[CONV_PAD + L:2 * CONV_PAD + L, :] = zero_pad
    hp_s[CONV_PAD:CONV_PAD + L, :] = zc[:, 0:GROUP_W] * _sigmoid(zc[:, GROUP_W:])

    zcs = _dot(zd.astype(BF16), bdcs_ref[...])
    zc_s[...] = zcs[:, 0:GROUP_W].astype(BF16)
    zs_s[...] = zcs[:, GROUP_W:].astype(BF16)

    qf = q_s[...]
    o = jnp.zeros((L, GROUP_W), F32)
    for hh in range(A_HEADS):
        oh = _attention_head(qf, k_s[...], v_s[...], hh, lam, lane)
        hm = (lane >= hh * A_V_DIM) & (lane < (hh + 1) * A_V_DIM)
        o = jnp.where(hm, oh, o)
    o_a = _group_rms(o, gmat64, vec(_V_GHEAD)) * (1.0 - lam_init)
    mix_s[:, 0:GROUP_W] = o_a.astype(BF16)

    for c in range(L // CHUNK):
        crow = slice(c * CHUNK, (c + 1) * CHUNK)
        mixed = _spatial_gate(_dot(wsp_ref[...], vn_s[crow, :]), bsp_ref[...], lane)
        mix_s[crow, GROUP_W:2 * GROUP_W] = (ub_s[crow, :] * mixed).astype(BF16)

    yc = _layer_norm(_depthwise_conv(hp_s[...], wdw_ref, vec(_V_BDW), L),
                     vec(_V_GCONV), vec(_V_BCONV))
    mix_s[:, 2 * GROUP_W:3 * GROUP_W] = (yc * _sigmoid(yc)).astype(BF16)

    o_d = _dot(csl_ref[:, 0:L], zc_s[...]) + _dot(csl_ref[:, L:2 * L], zs_s[...])
    mix_s[:, 3 * GROUP_W:4 * GROUP_W] = o_d.astype(BF16)

    x1 = x + gt1 * _dot(mix_s[...], w_out_ref[...])
    h2 = (_rms(x1, g_mlp) * (1.0 + sc2) + sh2).astype(BF16)
    f = jnp.maximum(_dot(h2, w1_ref[...]), 0.0)
    f = (f * f).astype(BF16)
    obuf[slot] = x1 + gt2 * _dot(f, w2_ref[...])

    out_copy(s % n_seq, slot).start()

    @pl.when(s == n_items - 1)
    def _():
        out_copy(0, slot).wait()
        out_copy(0, other).wait()


def _context_trunk(x, mod_all, params, consts, weights0, weights_f32):
    n_seq, L, _ = x.shape
    n_items = DEPTH * n_seq
    assert n_seq >= 4
    bdcs, csl = consts

    def layer_spec(arr):
        nd = arr.ndim - 1
        return pl.BlockSpec((None,) + arr.shape[1:],
                            lambda s, _n=nd: (s // n_seq,) + (0,) * _n)

    def const_spec(arr):
        return pl.BlockSpec(arr.shape, lambda s, _n=arr.ndim: (0,) * _n,
                            pipeline_mode=pl.Buffered(1))

    def next_layer_slice(s):
        layer = s // n_seq
        more = layer + 1 < DEPTH
        return (jnp.minimum(layer + 1, DEPTH - 1), jnp.where(more, s % n_seq, n_seq - 1), 0)

    def slice_in_spec(w):
        return pl.BlockSpec((None, w.shape[1] // n_seq, w.shape[2]), next_layer_slice)

    def slice_out_spec(w):
        def index(s):
            layer, piece, _ = next_layer_slice(s)
            return (layer - 1, piece, 0)
        return pl.BlockSpec((None, w.shape[1] // n_seq, w.shape[2]), index)

    any_spec = pl.BlockSpec(memory_space=pl.ANY)
    in_specs = ([any_spec, layer_spec(mod_all)]
                + [layer_spec(p) for p in params]
                + [const_spec(bdcs), const_spec(csl)]
                + [any_spec] * len(weights0)
                + [slice_in_spec(w) for w in weights_f32])
    args = [x, mod_all, *params, bdcs, csl, *weights0, *weights_f32]
    kv_out = pl.BlockSpec((None, None, L, GROUP_W), lambda s: (s % n_seq, s // n_seq, 0, 0))
    out_specs = [any_spec, kv_out, kv_out] + [slice_out_spec(w) for w in weights_f32]
    out_shape = [jax.ShapeDtypeStruct(x.shape, F32)] + [
        jax.ShapeDtypeStruct((n_seq, DEPTH, L, GROUP_W), F32)] * 2 + [
        jax.ShapeDtypeStruct((DEPTH - 1,) + w.shape[1:], BF16) for w in weights_f32]
    scratch = [
        pltpu.VMEM((2, L, D_MODEL), F32),
        pltpu.VMEM((2, L, D_MODEL), F32),
        pltpu.VMEM((L, GROUP_W), F32),
        pltpu.VMEM((L, GROUP_W), BF16),
        pltpu.VMEM((L, GROUP_W), BF16),
        pltpu.VMEM((L, GROUP_W), F32),
        pltpu.VMEM((L, GROUP_W), BF16),
        pltpu.VMEM((L + 2 * CONV_PAD, GROUP_W), F32),
        pltpu.VMEM((L, GROUP_W), BF16),
        pltpu.VMEM((L, GROUP_W), BF16),
        pltpu.VMEM((L, D_MODEL), BF16),
    ] + [pltpu.VMEM((2,) + w.shape[1:], BF16) for w in weights_f32] + [
        pltpu.SemaphoreType.DMA((2,)),
        pltpu.SemaphoreType.DMA((2,)),
        pltpu.SemaphoreType.DMA((len(weights0),)),
    ]
    kern = functools.partial(_ctx_kernel, n_seq=n_seq, seq_len=L)
    return pl.pallas_call(
        kern,
        grid=(n_items,),
        in_specs=in_specs,
        out_specs=out_specs,
        out_shape=out_shape,
        scratch_shapes=scratch,
        compiler_params=pltpu.CompilerParams(
            dimension_semantics=("arbitrary",),
            vmem_limit_bytes=BIG_VMEM_LIMIT),
        name="context_trunk",
    )(*args)


def _latent_kernel(x_ref, mod_ref, v1024_ref, v256_ref, lam_ref, wsp_ref, bsp_ref, wdw_ref,
                   bdcs_ref, csl_hbm, w_in_ref, w_out_hbm, w1_hbm, w2_hbm,
                   ck_ref, cv_ref, cos_ref, sin_ref,
                   xo_ref,
                   q_s, k_s, v_s, ub_s, vn_s, hp_s, zc_s, zs_s, mix_s,
                   csl_ref, w_out_ref, w1_ref, w2_ref, late_sem, *, seq_len, past_len, w_index):
    L = seq_len
    n_blk = L // ROW_BLOCK

    late_copies = [
        pltpu.make_async_copy(src, dst, late_sem.at[i])
        for i, (src, dst) in enumerate((
            (csl_hbm, csl_ref), (w_out_hbm.at[w_index], w_out_ref),
            (w1_hbm.at[w_index], w1_ref), (w2_hbm.at[w_index], w2_ref)))]

    @pl.when(pl.program_id(0) == 0)
    def _():
        for cp in late_copies:
            cp.start()

    mod = mod_ref[pl.ds(1 + pl.program_id(0), 1), :]
    sh1, sc1, gt1, sh2, sc2, gt2 = [
        mod[:, i * D_MODEL:(i + 1) * D_MODEL] for i in range(6)]
    g_attn = v1024_ref[0:1, :]
    g_mlp = v1024_ref[1:2, :]

    def vec(i):
        return v256_ref[i:i + 1, :]

    lane = lax.broadcasted_iota(jnp.int32, (1, GROUP_W), 1)
    gmat32 = _group_mean_matrix(A_QK_DIM)
    gmat64 = _group_mean_matrix(A_V_DIM)
    lam, lam_init = _lambda(lam_ref)
    swap_lo = (lane // (A_QK_DIM // 4)) % 2 == 0

    zero_pad = jnp.zeros((CONV_PAD, GROUP_W), F32)
    hp_s[0:CONV_PAD, :] = zero_pad
    hp_s[CONV_PAD + L:2 * CONV_PAD + L, :] = zero_pad
    k_s[0:past_len, :] = ck_ref[...].astype(BF16)
    v_s[0:past_len, :] = cv_ref[...].astype(BF16)

    def phase1(r0):
        rows = pl.ds(r0, ROW_BLOCK)
        x = x_ref[0, rows, :]
        h = _rms(x, g_attn) * (1.0 + sc1) + sh1
        z = _dot(h.astype(BF16), w_in_ref[...])
        zq = z[:, 0:GROUP_W]
        zk = z[:, GROUP_W:2 * GROUP_W]
        zv = z[:, 2 * GROUP_W:3 * GROUP_W]
        zb = z[:, 3 * GROUP_W:5 * GROUP_W]
        zc = z[:, 5 * GROUP_W:7 * GROUP_W]
        zd = z[:, 7 * GROUP_W:8 * GROUP_W]

        cs = cos_ref[rows, :]
        sn = sin_ref[rows, :]

        def rope(t):
            swapped = jnp.where(
                swap_lo,
                pltpu.roll(t, GROUP_W - A_QK_DIM // 4, axis=1),
                pltpu.roll(t, A_QK_DIM // 4, axis=1))
            return t * cs + swapped * sn

        q = rope(_group_rms(zq, gmat32, vec(_V_GQ)))
        k = rope(_group_rms(zk, gmat32, vec(_V_GK)))
        q_s[rows, :] = q * Q_SCALE
        k_s[pl.ds(past_len + r0, ROW_BLOCK), :] = k.astype(BF16)
        v_s[pl.ds(past_len + r0, ROW_BLOCK), :] = zv.astype(BF16)

        gb = jax.nn.gelu(zb, approximate=True)
        ub_s[rows, :] = gb[:, 0:GROUP_W]
        vn_s[rows, :] = _layer_norm(
            gb[:, GROUP_W:], vec(_V_GSG), vec(_V_BSG)).astype(BF16)

        hp_s[pl.ds(CONV_PAD + r0, ROW_BLOCK), :] = (
            zc[:, 0:GROUP_W] * _sigmoid(zc[:, GROUP_W:]))

        zcs = _dot(zd.astype(BF16), bdcs_ref[...])
        zc_s[rows, :] = zcs[:, 0:GROUP_W].astype(BF16)
        zs_s[rows, :] = zcs[:, GROUP_W:].astype(BF16)

    def phase2(r0):
        rows = pl.ds(r0, ROW_BLOCK)

        qf = q_s[rows, :]

        def head(hh, o):
            oh = _attention_head(qf, k_s[...], v_s[...], hh, lam, lane)
            hm = (lane >= hh * A_V_DIM) & (lane < (hh + 1) * A_V_DIM)
            return jnp.where(hm, oh, o)

        o = lax.fori_loop(0, A_HEADS, head, jnp.zeros((ROW_BLOCK, GROUP_W), F32),
                          unroll=True)
        o_a = _group_rms(o, gmat64, vec(_V_GHEAD)) * (1.0 - lam_init)
        mix_s[rows, 0:GROUP_W] = o_a.astype(BF16)

        for c in range(ROW_BLOCK // CHUNK):
            crow = pl.ds(r0 + c * CHUNK, CHUNK)
            mixed = _spatial_gate(_dot(wsp_ref[...], vn_s[crow, :]), bsp_ref[...], lane)
            mix_s[crow, GROUP_W:2 * GROUP_W] = (ub_s[crow, :] * mixed).astype(BF16)

        win = hp_s[pl.ds(r0, ROW_BLOCK + 2 * CONV_PAD), :]
        yc = _layer_norm(_depthwise_conv(win, wdw_ref, vec(_V_BDW), ROW_BLOCK),
                         vec(_V_GCONV), vec(_V_BCONV))
        mix_s[rows, 2 * GROUP_W:3 * GROUP_W] = (yc * _sigmoid(yc)).astype(BF16)

        o_d = (_dot(csl_ref[rows, 0:L], zc_s[...])
               + _dot(csl_ref[rows, L:2 * L], zs_s[...]))
        mix_s[rows, 3 * GROUP_W:4 * GROUP_W] = o_d.astype(BF16)

        x = x_ref[0, rows, :]
        x1 = x + gt1 * _dot(mix_s[rows, :], w_out_ref[...])
        h2 = (_rms(x1, g_mlp) * (1.0 + sc2) + sh2).astype(BF16)
        f = jnp.maximum(_dot(h2, w1_ref[...]), 0.0)
        f = (f * f).astype(BF16)
        xo_ref[0, rows, :] = x1 + gt2 * _dot(f, w2_ref[...])

    def for_each_block(phase):
        def step(r, carry):
            phase(pl.multiple_of(r * ROW_BLOCK, ROW_BLOCK))
            return carry
        lax.fori_loop(0, n_blk, step, 0)

    for_each_block(phase1)

    @pl.when(pl.program_id(0) == 0)
    def _():
        for cp in late_copies:
            cp.wait()

    for_each_block(phase2)


def _latent_layer(x, layer, mod_all, params, consts, weights, w_index,
                  cache_k, cache_v, cos_t, sin_t):
    n_seq, L, _ = x.shape
    past_len = cache_k.shape[2]
    bdcs, csl = consts
    single = dict(pipeline_mode=pl.Buffered(1))

    def layer_spec(arr, index=layer):
        nd = arr.ndim - 1
        return pl.BlockSpec((None,) + arr.shape[1:],
                            lambda b, _n=nd: (index,) + (0,) * _n, **single)

    def const_spec(arr):
        return pl.BlockSpec(arr.shape, lambda b, _n=arr.ndim: (0,) * _n, **single)

    x_spec = pl.BlockSpec((1, L, D_MODEL), lambda b: (b, 0, 0))
    kv_spec = pl.BlockSpec((None, None, past_len, GROUP_W), lambda b: (b, layer, 0, 0))
    any_spec = pl.BlockSpec(memory_space=pl.ANY)
    w_in, w_out, w1, w2 = weights
    in_specs = ([x_spec, layer_spec(mod_all)]
                + [layer_spec(p) for p in params]
                + [const_spec(bdcs), any_spec]
                + [layer_spec(w_in, w_index), any_spec, any_spec, any_spec]
                + [kv_spec, kv_spec, const_spec(cos_t), const_spec(sin_t)])
    args = [x, mod_all, *params, bdcs, csl, *weights, cache_k, cache_v, cos_t, sin_t]
    lk = past_len + L
    scratch = [
        pltpu.VMEM((L, GROUP_W), F32),
        pltpu.VMEM((lk, GROUP_W), BF16),
        pltpu.VMEM((lk, GROUP_W), BF16),
        pltpu.VMEM((L, GROUP_W), F32),
        pltpu.VMEM((L, GROUP_W), BF16),
        pltpu.VMEM((L + 2 * CONV_PAD, GROUP_W), F32),
        pltpu.VMEM((L, GROUP_W), BF16),
        pltpu.VMEM((L, GROUP_W), BF16),
        pltpu.VMEM((L, D_MODEL), BF16),
        pltpu.VMEM(csl.shape, BF16),
        pltpu.VMEM(w_out.shape[1:], BF16),
        pltpu.VMEM(w1.shape[1:], BF16),
        pltpu.VMEM(w2.shape[1:], BF16),
        pltpu.SemaphoreType.DMA((4,)),
    ]
    kern = functools.partial(_latent_kernel, seq_len=L, past_len=past_len, w_index=w_index)
    return pl.pallas_call(
        kern,
        grid=(n_seq,),
        in_specs=in_specs,
        out_specs=x_spec,
        out_shape=jax.ShapeDtypeStruct(x.shape, F32),
        scratch_shapes=scratch,
        compiler_params=pltpu.CompilerParams(
            dimension_semantics=("arbitrary",),
            vmem_limit_bytes=BIG_VMEM_LIMIT),
        name=f"latent_layer{layer}",
    )(*args)


def _dft_tables(L):
    nc = GROUP_W // D_GROUPS
    cc = np.arange(nc)
    ang_c = 2.0 * np.pi * ((cc[:, None] * cc[None, :]) % nc) / nc
    bd_c = np.kron(np.eye(D_GROUPS), np.cos(ang_c) / math.sqrt(nc))
    bd_s = np.kron(np.eye(D_GROUPS), np.sin(ang_c) / math.sqrt(nc))
    bdcs = np.concatenate([bd_c, bd_s], axis=1)
    pp = np.arange(L)
    ang_l = 2.0 * np.pi * ((pp[:, None] * pp[None, :]) % L) / L
    csl = np.concatenate([np.cos(ang_l), -np.sin(ang_l)], axis=1) / math.sqrt(L)
    return jnp.asarray(bdcs, F32).astype(BF16), jnp.asarray(csl, F32).astype(BF16)


def _rope_tables(n_tokens):
    rows = n_tokens // GRID_W
    row = np.repeat(np.arange(rows, dtype=np.float64), GRID_W)
    col = np.tile(np.arange(GRID_W, dtype=np.float64), rows)
    nf = A_QK_DIM // 4
    inv = ROPE_BASE ** (-np.arange(nf, dtype=np.float64) / nf)
    ang = np.stack([row[:, None] * inv, col[:, None] * inv], axis=1)
    cos, sin = np.cos(ang), np.sin(ang)
    cos32 = np.stack([cos, cos], axis=2).reshape(n_tokens, A_QK_DIM)
    sin32 = np.stack([-sin, sin], axis=2).reshape(n_tokens, A_QK_DIM)
    reps = GROUP_W // A_QK_DIM
    return (jnp.asarray(np.tile(cos32, (1, reps)), F32),
            jnp.asarray(np.tile(sin32, (1, reps)), F32))


def kernel(x_prompt, x_sample, c, cache_k, cache_v, c_ctx, w_ada, b_ada, g_attn_norm, g_mlp_norm, w_in, g_q, g_k, lam_q1, lam_k1, lam_q2, lam_k2, g_head, g_sg, b_sg, w_spatial, b_spatial, w_dw, b_dw, g_conv, b_conv, w_out, w_ff1, w_ff2):
    n_p, l_p, _ = x_prompt.shape
    n_s, l_s, _ = x_sample.shape
    past_len = cache_k.shape[2]

    cond = jnp.concatenate(
        [c_ctx[None, :], c, jnp.zeros((ADA_ROWS - 1 - n_s, D_MODEL), F32)], axis=0)
    mod_all = _ada_modulation(cond, w_ada, b_ada)

    weights_f32 = (w_in, w_out, w_ff1, w_ff2)
    weights0 = _cast_first_layer(weights_f32)

    bdcs, csl_p = _dft_tables(l_p)
    _, csl_s = _dft_tables(l_s)
    cos_t, sin_t = _rope_tables(l_s)
    ck4 = cache_k.reshape(n_s, DEPTH, past_len, GROUP_W)
    cv4 = cache_v.reshape(n_s, DEPTH, past_len, GROUP_W)

    v1024 = jnp.stack([g_attn_norm, g_mlp_norm], axis=1)
    rows = [jnp.tile(g_q, (1, GROUP_W // A_QK_DIM)), jnp.tile(g_k, (1, GROUP_W // A_QK_DIM)),
            jnp.tile(g_head, (1, GROUP_W // A_V_DIM)), g_sg, b_sg, b_dw, g_conv, b_conv]
    v256 = jnp.concatenate(
        [jnp.stack(rows, axis=1), jnp.zeros((DEPTH, 16 - len(rows), GROUP_W), F32)], axis=1)
    lam_init = jnp.asarray(
        [0.8 - 0.6 * math.exp(-0.3 * l) for l in range(DEPTH)], F32)
    lam5 = jnp.stack([lam_q1, lam_k1, lam_q2, lam_k2,
                      jnp.broadcast_to(lam_init[:, None], lam_q1.shape)], axis=1)
    lam5 = jnp.concatenate([lam5, jnp.zeros((DEPTH, 3, A_QK_DIM), F32)], axis=1)
    wsp = w_spatial.reshape(DEPTH, B_GROUPS * CHUNK, CHUNK).astype(BF16)
    bsp = jnp.repeat(jnp.swapaxes(b_spatial, 1, 2), GROUP_W // B_GROUPS, axis=2)
    wdw = jnp.concatenate([w_dw, jnp.zeros((DEPTH, 1, GROUP_W), F32)], axis=1)
    params = (v1024, v256, lam5, wsp, bsp, wdw)

    xp, new_k, new_v, *weights_rest = _context_trunk(
        x_prompt, mod_all, params, (bdcs, csl_p), weights0, weights_f32)
    xs = x_sample
    for l in range(DEPTH):
        if l == 0:
            weights, w_index = tuple(w[None] for w in weights0), 0
        else:
            weights, w_index = tuple(weights_rest), l - 1
        xs = _latent_layer(xs, l, mod_all, params, (bdcs, csl_s), weights, w_index,
                           ck4, cv4, cos_t, sin_t)

    new_k = new_k.reshape(n_p, DEPTH, l_p, A_HEADS, 2 * A_QK_DIM)
    new_v = new_v.reshape(n_p, DEPTH, l_p, A_HEADS, A_V_DIM)
    return (xp, xs, new_k, new_v)
```

```python
import functools
import math

import numpy as np
import jax
import jax.numpy as jnp
from jax import lax
from jax.experimental import pallas as pl
from jax.experimental.pallas import tpu as pltpu

F32 = jnp.float32
BF16 = jnp.bfloat16

D_MODEL = 1024
DEPTH = 4
GRID_W = 64
GROUP_W = 256
A_HEADS = 4
A_V_DIM = 64
A_QK_DIM = 32
B_GROUPS = 4
CHUNK = 128
C_KERNEL = 31
D_GROUPS = 4
D_FF = 4 * D_MODEL
ROPE_BASE = 10000.0
EPS = 1e-6
IN_W = 8 * GROUP_W
LOG2_E = 1.4426950408889634

ROW_BLOCK = 256
CONV_PAD = 16
ADA_ROWS = 8
ADA_COLS = 2048
CAST_STEPS = 8
VMEM_LIMIT = 58 * 1024 * 1024
BIG_VMEM_LIMIT = 62 * 1024 * 1024

_V_GQ, _V_GK, _V_GHEAD, _V_GSG, _V_BSG, _V_BDW, _V_GCONV, _V_BCONV = range(8)

Q_SCALE = A_QK_DIM ** -0.5 * LOG2_E


def _dot(a, b):
    return jnp.dot(a, b, preferred_element_type=F32)


def _sigmoid(x):
    return 1.0 / (1.0 + jnp.exp(-x))


def _rms(x, g):
    ms = jnp.mean(x * x, axis=-1, keepdims=True)
    return x * lax.rsqrt(ms + EPS) * g


def _layer_norm(x, g, b):
    mu = jnp.mean(x, axis=-1, keepdims=True)
    xc = x - mu
    var = jnp.mean(xc * xc, axis=-1, keepdims=True)
    return xc * lax.rsqrt(var + EPS) * g + b


def _group_rms(x, gmat, g):
    ms = _dot((x * x).astype(BF16), gmat)
    return x * lax.rsqrt(ms + EPS) * g


def _group_mean_matrix(gsize):
    row_i = lax.broadcasted_iota(jnp.int32, (GROUP_W, GROUP_W), 0)
    col_i = lax.broadcasted_iota(jnp.int32, (GROUP_W, GROUP_W), 1)
    return jnp.where(row_i // gsize == col_i // gsize, 1.0 / gsize, 0.0).astype(BF16)


def _lambda(lam_ref):
    lam_v = lam_ref[...]
    l1 = jnp.sum(lam_v[0:1] * lam_v[1:2], axis=-1, keepdims=True)
    l2 = jnp.sum(lam_v[2:3] * lam_v[3:4], axis=-1, keepdims=True)
    lam_init = lam_v[4:5, 0:1]
    return jnp.exp(l1) - jnp.exp(l2) + lam_init, lam_init


def _attention_head(qf, kb, vb, hh, lam, lane):
    outs = []
    for c in range(2):
        lo = hh * A_V_DIM + c * A_QK_DIM
        m = (lane >= lo) & (lane < lo + A_QK_DIM)
        qm = jnp.where(m, qf, 0.0).astype(BF16)
        s = lax.dot_general(qm, kb, (((1,), (1,)), ((), ())),
                            preferred_element_type=F32)
        e = jnp.exp2(s - jnp.max(s, axis=-1, keepdims=True))
        inv = 1.0 / jnp.sum(e, axis=-1, keepdims=True)
        outs.append(_dot(e.astype(BF16), vb) * inv)
    return outs[0] - lam * outs[1]


def _spatial_gate(res, bias, lane):
    mixed = bias
    for g in range(B_GROUPS):
        gm = (lane >= g * (GROUP_W // B_GROUPS)) & (lane < (g + 1) * (GROUP_W // B_GROUPS))
        mixed = mixed + jnp.where(gm, res[g * CHUNK:(g + 1) * CHUNK, :], 0.0)
    return mixed


def _depthwise_conv(win, wdw_ref, bias, n_rows):
    acc = jnp.zeros((n_rows, GROUP_W), F32) + bias
    for s_off in range(8):
        part = None
        for m_off in range(0, 2 * CONV_PAD, 8):
            j = m_off + s_off - 1
            if j < 0 or j >= C_KERNEL:
                continue
            term = win[m_off:m_off + n_rows + 8, :] * wdw_ref[j:j + 1, :]
            part = term if part is None else part + term
        acc = acc + part[s_off:s_off + n_rows, :]
    return acc


def _ada_kernel(cond_ref, w_ref, b_ref, o_ref):
    cnd = cond_ref[...]
    s = (cnd * _sigmoid(cnd)).astype(BF16)
    o_ref[0] = _dot(s, w_ref[0].astype(BF16)) + b_ref[0]


def _ada_modulation(cond, w_ada, b_ada):
    ncol = w_ada.shape[-1]
    return pl.pallas_call(
        _ada_kernel,
        grid=(DEPTH, ncol // ADA_COLS),
        in_specs=[
            pl.BlockSpec((ADA_ROWS, D_MODEL), lambda l, j: (0, 0)),
            pl.BlockSpec((1, D_MODEL, ADA_COLS), lambda l, j: (l, 0, j)),
            pl.BlockSpec((1, 1, ADA_COLS), lambda l, j: (l, 0, j)),
        ],
        out_specs=pl.BlockSpec((1, ADA_ROWS, ADA_COLS), lambda l, j: (l, 0, j)),
        out_shape=jax.ShapeDtypeStruct((DEPTH, ADA_ROWS, ncol), F32),
        compiler_params=pltpu.CompilerParams(
            dimension_semantics=("arbitrary", "arbitrary"),
            vmem_limit_bytes=VMEM_LIMIT),
        name="ada_modulation",
    )(cond, w_ada, b_ada.reshape(DEPTH, 1, ncol))


def _cast_kernel(*refs):
    n = len(refs) // 2
    for src, dst in zip(refs[:n], refs[n:]):
        dst[...] = src[...].astype(BF16)


def _cast_first_layer(weights_f32):
    def in_spec(w):
        return pl.BlockSpec((None, w.shape[1] // CAST_STEPS, w.shape[2]), lambda i: (0, i, 0))

    def out_spec(w):
        return pl.BlockSpec((None, w.shape[1] // CAST_STEPS, w.shape[2]), lambda i: (0, i, 0))

    return pl.pallas_call(
        _cast_kernel,
        grid=(CAST_STEPS,),
        in_specs=[in_spec(w) for w in weights_f32],
        out_specs=[out_spec(w) for w in weights_f32],
        out_shape=[jax.ShapeDtypeStruct((1,) + w.shape[1:], BF16) for w in weights_f32],
        compiler_params=pltpu.CompilerParams(
            dimension_semantics=("arbitrary",),
            vmem_limit_bytes=VMEM_LIMIT),
        name="cast_first_layer",
    )(*weights_f32)


def _ctx_kernel(xp_hbm, mod_ref, v1024_ref, v256_ref, lam_ref, wsp_ref, bsp_ref, wdw_ref,
                bdcs_ref, csl_ref,
                w_in0_hbm, w_out0_hbm, w1_0_hbm, w2_0_hbm,
                w_in_f32, w_out_f32, w1_f32, w2_f32,
                xo_hbm, nk_ref, nv_ref, w_in_o, w_out_o, w1_o, w2_o,
                xbuf, obuf, q_s, k_s, v_s, ub_s, vn_s, hp_s, zc_s, zs_s, mix_s,
                w_in_s, w_out_s, w1_s, w2_s,
                in_sem, out_sem, w_sem, *, n_seq, seq_len):
    L = seq_len
    n_items = DEPTH * n_seq
    s = pl.program_id(0)
    slot = s % 2
    other = 1 - slot

    w_cur = (s // n_seq) % 2
    w_nxt = 1 - w_cur
    chunk = s % n_seq
    layer0 = ((w_in0_hbm, w_in_s), (w_out0_hbm, w_out_s), (w1_0_hbm, w1_s), (w2_0_hbm, w2_s))

    @pl.when(s == 0)
    def _():
        copies = [pltpu.make_async_copy(src.at[0], dst.at[0], w_sem.at[i])
                  for i, (src, dst) in enumerate(layer0)]
        for cp in copies:
            cp.start()
        for cp in copies:
            cp.wait()

    for f32_ref, out_ref, dst in ((w_in_f32, w_in_o, w_in_s), (w_out_f32, w_out_o, w_out_s),
                                  (w1_f32, w1_o, w1_s), (w2_f32, w2_o, w2_s)):
        rows = f32_ref.shape[0]
        piece = f32_ref[...].astype(BF16)
        out_ref[...] = piece
        dst[w_nxt, pl.ds(pl.multiple_of(chunk * rows, rows), rows), :] = piece

    w_in_ref = w_in_s.at[w_cur]
    w_out_ref = w_out_s.at[w_cur]
    w1_ref = w1_s.at[w_cur]
    w2_ref = w2_s.at[w_cur]

    def fetch_copy(src_hbm, seq, dst_slot):
        return pltpu.make_async_copy(src_hbm.at[seq], xbuf.at[dst_slot], in_sem.at[dst_slot])

    def start_fetch(item, dst_slot):
        seq = item % n_seq

        @pl.when(item < n_seq)
        def _():
            fetch_copy(xp_hbm, seq, dst_slot).start()

        @pl.when(item >= n_seq)
        def _():
            fetch_copy(xo_hbm, seq, dst_slot).start()

    def out_copy(seq, src_slot):
        return pltpu.make_async_copy(obuf.at[src_slot], xo_hbm.at[seq], out_sem.at[src_slot])

    @pl.when(s == 0)
    def _():
        start_fetch(s, slot)

    fetch_copy(xp_hbm, 0, slot).wait()

    @pl.when(s + 1 < n_items)
    def _():
        start_fetch(s + 1, other)

    @pl.when(s >= 2)
    def _():
        out_copy(0, slot).wait()

    mod = mod_ref[0:1, :]
    sh1, sc1, gt1, sh2, sc2, gt2 = [
        mod[:, i * D_MODEL:(i + 1) * D_MODEL] for i in range(6)]
    g_attn = v1024_ref[0:1, :]
    g_mlp = v1024_ref[1:2, :]

    def vec(i):
        return v256_ref[i:i + 1, :]

    lane = lax.broadcasted_iota(jnp.int32, (1, GROUP_W), 1)
    gmat32 = _group_mean_matrix(A_QK_DIM)
    gmat64 = _group_mean_matrix(A_V_DIM)
    lam, lam_init = _lambda(lam_ref)

    x = xbuf[slot]
    hb = (_rms(x, g_attn) * (1.0 + sc1) + sh1).astype(BF16)

    def project(lo, hi):
        return _dot(hb, w_in_ref[:, lo * GROUP_W:hi * GROUP_W])

    zc = project(5, 7)
    zero_pad = jnp.zeros((CONV_PAD, GROUP_W), F32)
    hp_s[0:CONV_PAD, :] = zero_pad
    hp_s[CONV_PAD + L:2 * CONV_PAD + L, :] = zero_pad
    hp_s[CONV_PAD:CONV_PAD + L, :] = zc[:, 0:GROUP_W] * _sigmoid(zc[:, GROUP_W:])

    zb = project(3, 5)

    yc = _layer_norm(_depthwise_conv(hp_s[...], wdw_ref, vec(_V_BDW), L),
                     vec(_V_GCONV), vec(_V_BCONV))
    mix_s[:, 2 * GROUP_W:3 * GROUP_W] = (yc * _sigmoid(yc)).astype(BF16)

    zqkv = project(0, 3)

    gb = jax.nn.gelu(zb, approximate=True)
    ub_s[...] = gb[:, 0:GROUP_W]
    vn_s[...] = _layer_norm(gb[:, GROUP_W:], vec(_V_GSG), vec(_V_BSG)).astype(BF16)

    zd = project(7, 8)

    zq = zqkv[:, 0:GROUP_W]
    zk = zqkv[:, GROUP_W:2 * GROUP_W]
    zv = zqkv[:, 2 * GROUP_W:3 * GROUP_W]
    q = _group_rms(zq, gmat32, vec(_V_GQ))
    k = _group_rms(zk, gmat32, vec(_V_GK))
    nk_ref[...] = k
    nv_ref[...] = zv
    q_s[...] = q * Q_SCALE
    k_s[...] = k.astype(BF16)
    v_s[...] = zv.astype(BF16)

    zcs = _dot(zd.astype(BF16), bdcs_ref[...])
    zc_s[...] = zcs[:, 0:GROUP_W].astype(BF16)
    zs_s[...] = zcs[:, GROUP_W:].astype(BF16)

    qf = q_s[...]
    o = jnp.zeros((L, GROUP_W), F32)
    for hh in range(A_HEADS):
        oh = _attention_head(qf, k_s[...], v_s[...], hh, lam, lane)
        hm = (lane >= hh * A_V_DIM) & (lane < (hh + 1) * A_V_DIM)
        o = jnp.where(hm, oh, o)
    o_a = _group_rms(o, gmat64, vec(_V_GHEAD)) * (1.0 - lam_init)
    mix_s[:, 0:GROUP_W] = o_a.astype(BF16)

    for c in range(L // CHUNK):
        crow = slice(c * CHUNK, (c + 1) * CHUNK)
        mixed = _spatial_gate(_dot(wsp_ref[...], vn_s[crow, :]), bsp_ref[...], lane)
        mix_s[crow, GROUP_W:2 * GROUP_W] = (ub_s[crow, :] * mixed).astype(BF16)

    o_d = _dot(csl_ref[:, 0:L], zc_s[...]) + _dot(csl_ref[:, L:2 * L], zs_s[...])
    mix_s[:, 3 * GROUP_W:4 * GROUP_W] = o_d.astype(BF16)

    x1 = x + gt1 * _dot(mix_s[...], w_out_ref[...])
    h2 = (_rms(x1, g_mlp) * (1.0 + sc2) + sh2).astype(BF16)
    f = jnp.maximum(_dot(h2, w1_ref[...]), 0.0)
    f = (f * f).astype(BF16)
    obuf[slot] = x1 + gt2 * _dot(f, w2_ref[...])

    out_copy(s % n_seq, slot).start()

    @pl.when(s == n_items - 1)
    def _():
        out_copy(0, slot).wait()
        out_copy(0, other).wait()


def _context_trunk(x, mod_all, params, consts, weights0, weights_f32):
    n_seq, L, _ = x.shape
    n_items = DEPTH * n_seq
    assert n_seq >= 4
    bdcs, csl = consts

    def layer_spec(arr):
        nd = arr.ndim - 1
        return pl.BlockSpec((None,) + arr.shape[1:],
                            lambda s, _n=nd: (s // n_seq,) + (0,) * _n)

    def const_spec(arr):
        return pl.BlockSpec(arr.shape, lambda s, _n=arr.ndim: (0,) * _n,
                            pipeline_mode=pl.Buffered(1))

    def next_layer_slice(s):
        layer = s // n_seq
        more = layer + 1 < DEPTH
        return (jnp.minimum(layer + 1, DEPTH - 1), jnp.where(more, s % n_seq, n_seq - 1), 0)

    def slice_in_spec(w):
        return pl.BlockSpec((None, w.shape[1] // n_seq, w.shape[2]), next_layer_slice)

    def slice_out_spec(w):
        def index(s):
            layer, piece, _ = next_layer_slice(s)
            return (layer - 1, piece, 0)
        return pl.BlockSpec((None, w.shape[1] // n_seq, w.shape[2]), index)

    any_spec = pl.BlockSpec(memory_space=pl.ANY)
    in_specs = ([any_spec, layer_spec(mod_all)]
                + [layer_spec(p) for p in params]
                + [const_spec(bdcs), const_spec(csl)]
                + [any_spec] * len(weights0)
                + [slice_in_spec(w) for w in weights_f32])
    args = [x, mod_all, *params, bdcs, csl, *weights0, *weights_f32]
    kv_out = pl.BlockSpec((None, None, L, GROUP_W), lambda s: (s % n_seq, s // n_seq, 0, 0))
    out_specs = [any_spec, kv_out, kv_out] + [slice_out_spec(w) for w in weights_f32]
    out_shape = [jax.ShapeDtypeStruct(x.shape, F32)] + [
        jax.ShapeDtypeStruct((n_seq, DEPTH, L, GROUP_W), F32)] * 2 + [
        jax.ShapeDtypeStruct((DEPTH - 1,) + w.shape[1:], BF16) for w in weights_f32]
    scratch = [
        pltpu.VMEM((2, L, D_MODEL), F32),
        pltpu.VMEM((2, L, D_MODEL), F32),
        pltpu.VMEM((L, GROUP_W), F32),
        pltpu.VMEM((L, GROUP_W), BF16),
        pltpu.VMEM((L, GROUP_W), BF16),
        pltpu.VMEM((L, GROUP_W), F32),
        pltpu.VMEM((L, GROUP_W), BF16),
        pltpu.VMEM((L + 2 * CONV_PAD, GROUP_W), F32),
        pltpu.VMEM((L, GROUP_W), BF16),
        pltpu.VMEM((L, GROUP_W), BF16),
        pltpu.VMEM((L, D_MODEL), BF16),
    ] + [pltpu.VMEM((2,) + w.shape[1:], BF16) for w in weights_f32] + [
        pltpu.SemaphoreType.DMA((2,)),
        pltpu.SemaphoreType.DMA((2,)),
        pltpu.SemaphoreType.DMA((len(weights0),)),
    ]
    kern = functools.partial(_ctx_kernel, n_seq=n_seq, seq_len=L)
    return pl.pallas_call(
        kern,
        grid=(n_items,),
        in_specs=in_specs,
        out_specs=out_specs,
        out_shape=out_shape,
        scratch_shapes=scratch,
        compiler_params=pltpu.CompilerParams(
            dimension_semantics=("arbitrary",),
            vmem_limit_bytes=BIG_VMEM_LIMIT),
        name="context_trunk",
    )(*args)


def _latent_kernel(x_ref, mod_ref, v1024_ref, v256_ref, lam_ref, wsp_ref, bsp_ref, wdw_ref,
                   bdcs_ref, csl_hbm, w_in_ref, w_out_hbm, w1_hbm, w2_hbm,
                   ck_ref, cv_ref, cos_ref, sin_ref,
                   xo_ref,
                   q_s, k_s, v_s, ub_s, vn_s, hp_s, zc_s, zs_s, mix_s,
                   csl_ref, w_out_ref, w1_ref, w2_ref, late_sem, *, seq_len, past_len, w_index):
    L = seq_len
    n_blk = L // ROW_BLOCK

    late_copies = [
        pltpu.make_async_copy(src, dst, late_sem.at[i])
        for i, (src, dst) in enumerate((
            (csl_hbm, csl_ref), (w_out_hbm.at[w_index], w_out_ref),
            (w1_hbm.at[w_index], w1_ref), (w2_hbm.at[w_index], w2_ref)))]

    @pl.when(pl.program_id(0) == 0)
    def _():
        for cp in late_copies:
            cp.start()

    mod = mod_ref[pl.ds(1 + pl.program_id(0), 1), :]
    sh1, sc1, gt1, sh2, sc2, gt2 = [
        mod[:, i * D_MODEL:(i + 1) * D_MODEL] for i in range(6)]
    g_attn = v1024_ref[0:1, :]
    g_mlp = v1024_ref[1:2, :]

    def vec(i):
        return v256_ref[i:i + 1, :]

    lane = lax.broadcasted_iota(jnp.int32, (1, GROUP_W), 1)
    gmat32 = _group_mean_matrix(A_QK_DIM)
    gmat64 = _group_mean_matrix(A_V_DIM)
    lam, lam_init = _lambda(lam_ref)
    swap_lo = (lane // (A_QK_DIM // 4)) % 2 == 0

    zero_pad = jnp.zeros((CONV_PAD, GROUP_W), F32)
    hp_s[0:CONV_PAD, :] = zero_pad
    hp_s[CONV_PAD + L:2 * CONV_PAD + L, :] = zero_pad
    k_s[0:past_len, :] = ck_ref[...].astype(BF16)
    v_s[0:past_len, :] = cv_ref[...].astype(BF16)

    def phase1(r0):
        rows = pl.ds(r0, ROW_BLOCK)
        x = x_ref[0, rows, :]
        h = _rms(x, g_attn) * (1.0 + sc1) + sh1
        z = _dot(h.astype(BF16), w_in_ref[...])
        zq = z[:, 0:GROUP_W]
        zk = z[:, GROUP_W:2 * GROUP_W]
        zv = z[:, 2 * GROUP_W:3 * GROUP_W]
        zb = z[:, 3 * GROUP_W:5 * GROUP_W]
        zc = z[:, 5 * GROUP_W:7 * GROUP_W]
        zd = z[:, 7 * GROUP_W:8 * GROUP_W]

        cs = cos_ref[rows, :]
        sn = sin_ref[rows, :]

        def rope(t):
            swapped = jnp.where(
                swap_lo,
                pltpu.roll(t, GROUP_W - A_QK_DIM // 4, axis=1),
                pltpu.roll(t, A_QK_DIM // 4, axis=1))
            return t * cs + swapped * sn

        q = rope(_group_rms(zq, gmat32, vec(_V_GQ)))
        k = rope(_group_rms(zk, gmat32, vec(_V_GK)))
        q_s[rows, :] = q * Q_SCALE
        k_s[pl.ds(past_len + r0, ROW_BLOCK), :] = k.astype(BF16)
        v_s[pl.ds(past_len + r0, ROW_BLOCK), :] = zv.astype(BF16)

        gb = jax.nn.gelu(zb, approximate=True)
        ub_s[rows, :] = gb[:, 0:GROUP_W]
        vn_s[rows, :] = _layer_norm(
            gb[:, GROUP_W:], vec(_V_GSG), vec(_V_BSG)).astype(BF16)

        hp_s[pl.ds(CONV_PAD + r0, ROW_BLOCK), :] = (
            zc[:, 0:GROUP_W] * _sigmoid(zc[:, GROUP_W:]))

        zcs = _dot(zd.astype(BF16), bdcs_ref[...])
        zc_s[rows, :] = zcs[:, 0:GROUP_W].astype(BF16)
        zs_s[rows, :] = zcs[:, GROUP_W:].astype(BF16)

    def phase2(r0):
        rows = pl.ds(r0, ROW_BLOCK)

        qf = q_s[rows, :]

        def head(hh, o):
            oh = _attention_head(qf, k_s[...], v_s[...], hh, lam, lane)
            hm = (lane >= hh * A_V_DIM) & (lane < (hh + 1) * A_V_DIM)
            return jnp.where(hm, oh, o)

        o = lax.fori_loop(0, A_HEADS, head, jnp.zeros((ROW_BLOCK, GROUP_W), F32),
                          unroll=True)
        o_a = _group_rms(o, gmat64, vec(_V_GHEAD)) * (1.0 - lam_init)
        mix_s[rows, 0:GROUP_W] = o_a.astype(BF16)

        for c in range(ROW_BLOCK // CHUNK):
            crow = pl.ds(r0 + c * CHUNK, CHUNK)
            mixed = _spatial_gate(_dot(wsp_ref[...], vn_s[crow, :]), bsp_ref[...], lane)
            mix_s[crow, GROUP_W:2 * GROUP_W] = (ub_s[crow, :] * mixed).astype(BF16)

        win = hp_s[pl.ds(r0, ROW_BLOCK + 2 * CONV_PAD), :]
        yc = _layer_norm(_depthwise_conv(win, wdw_ref, vec(_V_BDW), ROW_BLOCK),
                         vec(_V_GCONV), vec(_V_BCONV))
        mix_s[rows, 2 * GROUP_W:3 * GROUP_W] = (yc * _sigmoid(yc)).astype(BF16)

        o_d = (_dot(csl_ref[rows, 0:L], zc_s[...])
               + _dot(csl_ref[rows, L:2 * L], zs_s[...]))
        mix_s[rows, 3 * GROUP_W:4 * GROUP_W] = o_d.astype(BF16)

        x = x_ref[0, rows, :]
        x1 = x + gt1 * _dot(mix_s[rows, :], w_out_ref[...])
        h2 = (_rms(x1, g_mlp) * (1.0 + sc2) + sh2).astype(BF16)
        f = jnp.maximum(_dot(h2, w1_ref[...]), 0.0)
        f = (f * f).astype(BF16)
        xo_ref[0, rows, :] = x1 + gt2 * _dot(f, w2_ref[...])

    def for_each_block(phase):
        def step(r, carry):
            phase(pl.multiple_of(r * ROW_BLOCK, ROW_BLOCK))
            return carry
        lax.fori_loop(0, n_blk, step, 0)

    for_each_block(phase1)

    @pl.when(pl.program_id(0) == 0)
    def _():
        for cp in late_copies:
            cp.wait()

    for_each_block(phase2)


def _latent_layer(x, layer, mod_all, params, consts, weights, w_index,
                  cache_k, cache_v, cos_t, sin_t):
    n_seq, L, _ = x.shape
    past_len = cache_k.shape[2]
    bdcs, csl = consts
    single = dict(pipeline_mode=pl.Buffered(1))

    def layer_spec(arr, index=layer):
        nd = arr.ndim - 1
        return pl.BlockSpec((None,) + arr.shape[1:],
                            lambda b, _n=nd: (index,) + (0,) * _n, **single)

    def const_spec(arr):
        return pl.BlockSpec(arr.shape, lambda b, _n=arr.ndim: (0,) * _n, **single)

    x_spec = pl.BlockSpec((1, L, D_MODEL), lambda b: (b, 0, 0))
    kv_spec = pl.BlockSpec((None, None, past_len, GROUP_W), lambda b: (b, layer, 0, 0))
    any_spec = pl.BlockSpec(memory_space=pl.ANY)
    w_in, w_out, w1, w2 = weights
    in_specs = ([x_spec, layer_spec(mod_all)]
                + [layer_spec(p) for p in params]
                + [const_spec(bdcs), any_spec]
                + [layer_spec(w_in, w_index), any_spec, any_spec, any_spec]
                + [kv_spec, kv_spec, const_spec(cos_t), const_spec(sin_t)])
    args = [x, mod_all, *params, bdcs, csl, *weights, cache_k, cache_v, cos_t, sin_t]
    lk = past_len + L
    scratch = [
        pltpu.VMEM((L, GROUP_W), F32),
        pltpu.VMEM((lk, GROUP_W), BF16),
        pltpu.VMEM((lk, GROUP_W), BF16),
        pltpu.VMEM((L, GROUP_W), F32),
        pltpu.VMEM((L, GROUP_W), BF16),
        pltpu.VMEM((L + 2 * CONV_PAD, GROUP_W), F32),
        pltpu.VMEM((L, GROUP_W), BF16),
        pltpu.VMEM((L, GROUP_W), BF16),
        pltpu.VMEM((L, D_MODEL), BF16),
        pltpu.VMEM(csl.shape, BF16),
        pltpu.VMEM(w_out.shape[1:], BF16),
        pltpu.VMEM(w1.shape[1:], BF16),
        pltpu.VMEM(w2.shape[1:], BF16),
        pltpu.SemaphoreType.DMA((4,)),
    ]
    kern = functools.partial(_latent_kernel, seq_len=L, past_len=past_len, w_index=w_index)
    return pl.pallas_call(
        kern,
        grid=(n_seq,),
        in_specs=in_specs,
        out_specs=x_spec,
        out_shape=jax.ShapeDtypeStruct(x.shape, F32),
        scratch_shapes=scratch,
        compiler_params=pltpu.CompilerParams(
            dimension_semantics=("arbitrary",),
            vmem_limit_bytes=BIG_VMEM_LIMIT),
        name=f"latent_layer{layer}",
    )(*args)


def _dft_tables(L):
    nc = GROUP_W // D_GROUPS
    cc = np.arange(nc)
    ang_c = 2.0 * np.pi * ((cc[:, None] * cc[None, :]) % nc) / nc
    bd_c = np.kron(np.eye(D_GROUPS), np.cos(ang_c) / math.sqrt(nc))
    bd_s = np.kron(np.eye(D_GROUPS), np.sin(ang_c) / math.sqrt(nc))
    bdcs = np.concatenate([bd_c, bd_s], axis=1)
    pp = np.arange(L)
    ang_l = 2.0 * np.pi * ((pp[:, None] * pp[None, :]) % L) / L
    csl = np.concatenate([np.cos(ang_l), -np.sin(ang_l)], axis=1) / math.sqrt(L)
    return jnp.asarray(bdcs, F32).astype(BF16), jnp.asarray(csl, F32).astype(BF16)


def _rope_tables(n_tokens):
    rows = n_tokens // GRID_W
    row = np.repeat(np.arange(rows, dtype=np.float64), GRID_W)
    col = np.tile(np.arange(GRID_W, dtype=np.float64), rows)
    nf = A_QK_DIM // 4
    inv = ROPE_BASE ** (-np.arange(nf, dtype=np.float64) / nf)
    ang = np.stack([row[:, None] * inv, col[:, None] * inv], axis=1)
    cos, sin = np.cos(ang), np.sin(ang)
    cos32 = np.stack([cos, cos], axis=2).reshape(n_tokens, A_QK_DIM)
    sin32 = np.stack([-sin, sin], axis=2).reshape(n_tokens, A_QK_DIM)
    reps = GROUP_W // A_QK_DIM
    return (jnp.asarray(np.tile(cos32, (1, reps)), F32),
            jnp.asarray(np.tile(sin32, (1, reps)), F32))


def kernel(x_prompt, x_sample, c, cache_k, cache_v, c_ctx, w_ada, b_ada, g_attn_norm, g_mlp_norm, w_in, g_q, g_k, lam_q1, lam_k1, lam_q2, lam_k2, g_head, g_sg, b_sg, w_spatial, b_spatial, w_dw, b_dw, g_conv, b_conv, w_out, w_ff1, w_ff2):
    n_p, l_p, _ = x_prompt.shape
    n_s, l_s, _ = x_sample.shape
    past_len = cache_k.shape[2]

    cond = jnp.concatenate(
        [c_ctx[None, :], c, jnp.zeros((ADA_ROWS - 1 - n_s, D_MODEL), F32)], axis=0)
    mod_all = _ada_modulation(cond, w_ada, b_ada)

    weights_f32 = (w_in, w_out, w_ff1, w_ff2)
    weights0 = _cast_first_layer(weights_f32)

    bdcs, csl_p = _dft_tables(l_p)
    _, csl_s = _dft_tables(l_s)
    cos_t, sin_t = _rope_tables(l_s)
    ck4 = cache_k.reshape(n_s, DEPTH, past_len, GROUP_W)
    cv4 = cache_v.reshape(n_s, DEPTH, past_len, GROUP_W)

    v1024 = jnp.stack([g_attn_norm, g_mlp_norm], axis=1)
    rows = [jnp.tile(g_q, (1, GROUP_W // A_QK_DIM)), jnp.tile(g_k, (1, GROUP_W // A_QK_DIM)),
            jnp.tile(g_head, (1, GROUP_W // A_V_DIM)), g_sg, b_sg, b_dw, g_conv, b_conv]
    v256 = jnp.concatenate(
        [jnp.stack(rows, axis=1), jnp.zeros((DEPTH, 16 - len(rows), GROUP_W), F32)], axis=1)
    lam_init = jnp.asarray(
        [0.8 - 0.6 * math.exp(-0.3 * l) for l in range(DEPTH)], F32)
    lam5 = jnp.stack([lam_q1, lam_k1, lam_q2, lam_k2,
                      jnp.broadcast_to(lam_init[:, None], lam_q1.shape)], axis=1)
    lam5 = jnp.concatenate([lam5, jnp.zeros((DEPTH, 3, A_QK_DIM), F32)], axis=1)
    wsp = w_spatial.reshape(DEPTH, B_GROUPS * CHUNK, CHUNK).astype(BF16)
    bsp = jnp.repeat(jnp.swapaxes(b_spatial, 1, 2), GROUP_W // B_GROUPS, axis=2)
    wdw = jnp.concatenate([w_dw, jnp.zeros((DEPTH, 1, GROUP_W), F32)], axis=1)
    params = (v1024, v256, lam5, wsp, bsp, wdw)

    xp, new_k, new_v, *weights_rest = _context_trunk(
        x_prompt, mod_all, params, (bdcs, csl_p), weights0, weights_f32)
    xs = x_sample
    for l in range(DEPTH):
        if l == 0:
            weights, w_index = weights0, 0
        else:
            weights, w_index = tuple(weights_rest), l - 1
        xs = _latent_layer(xs, l, mod_all, params, (bdcs, csl_s), weights, w_index,
                           ck4, cv4, cos_t, sin_t)

    new_k = new_k.reshape(n_p, DEPTH, l_p, A_HEADS, 2 * A_QK_DIM)
    new_v = new_v.reshape(n_p, DEPTH, l_p, A_HEADS, A_V_DIM)
    return (xp, xs, new_k, new_v)
```
